```python
import functools
import jax, jax.numpy as jnp
from jax import lax
import numpy as np

D_MODEL = 2048
BATCH = 4
SEQ = 2048
DEPTH = 4
DEC_BATCH = 128
DEC_SEQ = 8
PAST_LEN = 8192
PAGE_SIZE = 128

N_META = 16
N_EVEN = (DEPTH + 1) // 2
N_ODD = DEPTH // 2
EPS = 1e-6
ROPE_BASE = 10000.0

CONV_CH = 1024
CONV_WIDTH = 31
MLA_HEADS = 8
Q_RANK = 512
KV_RANK = 512
NOPE_DIM = 128
ROPE_DIM = 64
V_DIM = 128
QK_DIM = NOPE_DIM + ROPE_DIM
MLA_SCALE = QK_DIM ** -0.5
Q_BLOCK = 128
EVEN_IN = 2 * CONV_CH + Q_RANK + KV_RANK + ROPE_DIM
EVEN_MIX = CONV_CH + MLA_HEADS * V_DIM
RET_HEADS = 8
RET_DK = D_MODEL // RET_HEADS
RET_DV = 2 * RET_DK
RET_CHUNK = 128
RET_HK = RET_HEADS * RET_DK
RET_HV = RET_HEADS * RET_DV
ODD_IN = 2 * RET_HK + 2 * RET_HV
D_FF = -(-8 * D_MODEL // (3 * 256)) * 256

kernel_name = 'hybrid_conv_mla_retention_step'

F32 = jnp.float32


def rmsnorm(x, g):
    xf = x.astype(F32)
    y = xf * lax.rsqrt(jnp.mean(xf * xf, -1, keepdims=True) + EPS)
    return (y * g.astype(F32)).astype(x.dtype)


def layernorm(x, g, b):
    xf = x.astype(F32)
    mu = jnp.mean(xf, -1, keepdims=True)
    var = jnp.mean(jnp.square(xf - mu), -1, keepdims=True)
    return ((xf - mu) * lax.rsqrt(var + EPS) * g.astype(F32) + b.astype(F32)).astype(x.dtype)


def rope(x, pos):
    d = x.shape[-1]
    inv = ROPE_BASE ** (-jnp.arange(d // 2, dtype=F32) * 2.0 / d)
    ang = pos.astype(F32)[:, None] * inv
    ang = ang.reshape((ang.shape[0],) + (1,) * (x.ndim - 3) + (d // 2,))
    cos, sin = jnp.cos(ang), jnp.sin(ang)
    x1, x2 = x[..., :d // 2].astype(F32), x[..., d // 2:].astype(F32)
    return jnp.concatenate([x1 * cos - x2 * sin, x2 * cos + x1 * sin], -1).astype(x.dtype)


def swiglu(x, wg, wu, wd):
    return (jax.nn.silu(x @ wg) * (x @ wu)) @ wd


def conv_module(a_in, buf, conv_w, conv_b, ln_g, ln_b):
    a, gate = jnp.split(a_in, 2, axis=-1)
    u = a * jax.nn.sigmoid(gate)
    full = jnp.concatenate([buf.astype(u.dtype), u], 1)
    y = lax.conv_general_dilated(full, conv_w[:, None, :].astype(full.dtype), window_strides=(1,), padding='VALID',
                                 dimension_numbers=('NWC', 'WIO', 'NWC'), feature_group_count=CONV_CH) + conv_b
    y = jax.nn.silu(layernorm(y, ln_g, ln_b))
    return y, full[:, -(CONV_WIDTH - 1):]


def mla_block_attend(q_nope, q_pe, q_pos, k_nope, k_pe, v, k_pos):
    s = (jnp.einsum('bqhd,bkhd->bhqk', q_nope, k_nope) + jnp.einsum('bqhd,bkd->bhqk', q_pe, k_pe)).astype(F32) * MLA_SCALE
    s = jnp.where(k_pos[None, :] <= q_pos[:, None], s, -jnp.inf)
    p = jax.nn.softmax(s, axis=-1).astype(v.dtype)
    return jnp.einsum('bhqk,bkhe->bqhe', p, v)


def mla_prompt_attend(q_nope, q_pe, ckv, kpe, w_kv_up, pos):
    B, L = q_nope.shape[:2]
    k_nope = jnp.einsum('bnr,rhd->bnhd', ckv, w_kv_up[..., :NOPE_DIM])
    v = jnp.einsum('bnr,rhe->bnhe', ckv, w_kv_up[..., NOPE_DIM:])
    o_meta = mla_block_attend(q_nope[:, :N_META], q_pe[:, :N_META], pos[:N_META],
                              k_nope[:, :N_META], kpe[:, :N_META], v[:, :N_META], pos[:N_META])
    nb = (L - N_META) // Q_BLOCK

    def blocks(t):
        t = t[:, N_META:]
        return t.reshape((B, nb, Q_BLOCK) + t.shape[2:]).swapaxes(0, 1)

    o = lax.map(lambda a: mla_block_attend(a[0], a[1], a[2], k_nope, kpe, v, pos),
                (blocks(q_nope), blocks(q_pe), pos[N_META:].reshape(nb, Q_BLOCK)))
    o = o.swapaxes(0, 1).reshape(B, nb * Q_BLOCK, MLA_HEADS, V_DIM)
    return jnp.concatenate([o_meta, o], 1)


def mla_sample_attend(q_nope, q_pe, ckv, kpe, w_kv_up, cache_ckv, cache_kpe, page_table, li):
    T = q_nope.shape[1]
    w_uk, w_uv = w_kv_up[..., :NOPE_DIM], w_kv_up[..., NOPE_DIM:]
    q_lat = jnp.einsum('bthd,rhd->bthr', q_nope, w_uk)
    q_pos = PAST_LEN + jnp.arange(T)
    k_pos = jnp.arange(PAST_LEN + T)
    mask = k_pos[None, :] <= q_pos[:, None]

    def one(args):
        pt, ql, qp, cn, kn = args
        c = jnp.concatenate([cache_ckv[li, pt].reshape(-1, KV_RANK).astype(cn.dtype), cn], 0)
        kp = jnp.concatenate([cache_kpe[li, pt].reshape(-1, ROPE_DIM).astype(kn.dtype), kn], 0)
        s = (jnp.einsum('thr,sr->hts', ql, c) + jnp.einsum('thd,sd->hts', qp, kp)).astype(F32) * MLA_SCALE
        s = jnp.where(mask, s, -jnp.inf)
        p = jax.nn.softmax(s, axis=-1).astype(c.dtype)
        return jnp.einsum('hts,sr->thr', p, c)

    o_lat = lax.map(one, (page_table, q_lat, q_pe, ckv, kpe))
    return jnp.einsum('bthr,rhe->bthe', o_lat, w_uv)


def even_mixer(hn, pos, conv_buf, attend, w_in, conv_w, conv_b, ln_g, ln_b, q_norm, w_uq, kv_norm, w_out):
    B, n, _ = hn.shape
    z = hn @ w_in
    a_in, cq, ckv, kpe = jnp.split(z, [2 * CONV_CH, 2 * CONV_CH + Q_RANK, 2 * CONV_CH + Q_RANK + KV_RANK], axis=-1)
    conv_out, new_buf = conv_module(a_in, conv_buf, conv_w, conv_b, ln_g, ln_b)
    q = jnp.einsum('bnr,rhd->bnhd', rmsnorm(cq, q_norm), w_uq)
    q_nope, q_pe = q[..., :NOPE_DIM], rope(q[..., NOPE_DIM:], pos)
    ckv = rmsnorm(ckv, kv_norm)
    kpe = rope(kpe, pos)
    attn = attend(q_nope, q_pe, ckv, kpe)
    mixed = jnp.concatenate([conv_out, attn.reshape(B, n, MLA_HEADS * V_DIM)], -1)
    return mixed @ w_out, new_buf, ckv, kpe


def retention_log_decay():
    return jnp.log1p(-(2.0 ** (-5.0 - jnp.arange(RET_HEADS, dtype=F32))))


def retention_chunk(q, k, v, S, log_g):
    n = q.shape[1]
    i = jnp.arange(n, dtype=F32)
    rel = i[:, None] - i[None, :]
    decay = jnp.where(rel >= 0, jnp.exp(log_g[:, None, None] * jnp.maximum(rel, 0.0)), 0.0)
    inner = jnp.einsum('bihd,bjhd->bhij', q, k) * decay
    o = jnp.einsum('bhij,bjhe->bihe', inner, v)
    o = o + jnp.einsum('bihd,bhde->bihe', q, S) * jnp.exp((i[:, None] + 1.0) * log_g)[None, :, :, None]
    kw = k * jnp.exp((n - 1.0 - i)[:, None] * log_g)[None, :, :, None]
    S = jnp.exp(n * log_g)[None, :, None, None] * S + jnp.einsum('bjhd,bjhe->bhde', kw, v)
    return o, S


def retention_prompt_run(q, k, v, S0, log_g):
    B, L = q.shape[:2]
    o_meta, S = retention_chunk(q[:, :N_META], k[:, :N_META], v[:, :N_META], S0, log_g)
    nc = (L - N_META) // RET_CHUNK

    def chunks(t):
        t = t[:, N_META:]
        return t.reshape((B, nc, RET_CHUNK) + t.shape[2:]).swapaxes(0, 1)

    def step(S, xs):
        o, S = retention_chunk(xs[0], xs[1], xs[2], S, log_g)
        return S, o

    S, o = lax.scan(step, S, (chunks(q), chunks(k), chunks(v)))
    o = o.swapaxes(0, 1).reshape(B, nc * RET_CHUNK, RET_HEADS, RET_DV)
    return jnp.concatenate([o_meta, o], 1), S


def retention_mixer(hn, pos, S0, run, log_g, w_in, w_out):
    B, n, _ = hn.shape
    z = hn @ w_in
    q, k, v, g = jnp.split(z, [RET_HK, 2 * RET_HK, 2 * RET_HK + RET_HV], axis=-1)
    q = rope(q.reshape(B, n, RET_HEADS, RET_DK), pos).astype(F32)
    k = rope(k.reshape(B, n, RET_HEADS, RET_DK), pos).astype(F32) * (RET_DK ** -0.5)
    v = v.reshape(B, n, RET_HEADS, RET_DV).astype(F32)
    o, S = run(q, k, v, S0.astype(F32), log_g)
    mu = jnp.mean(o, -1, keepdims=True)
    var = jnp.mean(jnp.square(o - mu), -1, keepdims=True)
    o = (o - mu) * lax.rsqrt(var + EPS)
    y = jax.nn.silu(g) * o.reshape(B, n, RET_HV).astype(g.dtype)
    return y @ w_out, S


def setup_inputs(seed: int = 0) -> dict:
    key = jax.random.key(seed)
    ks = iter(jax.random.split(key, 40))

    def nrm(shape, scale):
        return jax.random.normal(next(ks), shape, F32) * scale

    n_pages = PAST_LEN // PAGE_SIZE
    n_used = DEC_BATCH * n_pages
    n_phys = n_used + (n_used + 3) // 4
    x_prompt = nrm((BATCH, SEQ, D_MODEL), 1.0)
    x_sample = nrm((DEC_BATCH, DEC_SEQ, D_MODEL), 1.0)
    cache_ckv = nrm((N_EVEN, n_phys, PAGE_SIZE, KV_RANK), 1.0)
    cache_kpe = nrm((N_EVEN, n_phys, PAGE_SIZE, ROPE_DIM), 1.0)
    state_conv = nrm((N_EVEN, DEC_BATCH, CONV_WIDTH - 1, CONV_CH), 0.5)
    state_ret = nrm((N_ODD, DEC_BATCH, RET_HEADS, RET_DK, RET_DV), 0.1)
    page_table = jax.random.permutation(next(ks), n_phys)[:n_used].reshape(DEC_BATCH, n_pages).astype(jnp.int32)
    return {
        'x_prompt': x_prompt,
        'x_sample': x_sample,
        'cache_ckv': cache_ckv,
        'cache_kpe': cache_kpe,
        'state_conv': state_conv,
        'state_ret': state_ret,
        'page_table': page_table,
        'meta_tokens': nrm((N_META, D_MODEL), 1.0),
        'norm_mix': 1.0 + nrm((DEPTH, D_MODEL), 0.02),
        'norm_ffn': 1.0 + nrm((DEPTH, D_MODEL), 0.02),
        'norm_final': 1.0 + nrm((D_MODEL,), 0.02),
        'w_in_even': nrm((N_EVEN, D_MODEL, EVEN_IN), D_MODEL ** -0.5),
        'conv_w': nrm((N_EVEN, CONV_WIDTH, CONV_CH), CONV_WIDTH ** -0.5),
        'conv_b': nrm((N_EVEN, CONV_CH), 0.02),
        'conv_ln_g': 1.0 + nrm((N_EVEN, CONV_CH), 0.02),
        'conv_ln_b': nrm((N_EVEN, CONV_CH), 0.02),
        'mla_q_norm': 1.0 + nrm((N_EVEN, Q_RANK), 0.02),
        'mla_w_uq': nrm((N_EVEN, Q_RANK, MLA_HEADS, QK_DIM), Q_RANK ** -0.5),
        'mla_kv_norm': 1.0 + nrm((N_EVEN, KV_RANK), 0.02),
        'mla_w_kv_up': nrm((N_EVEN, KV_RANK, MLA_HEADS, NOPE_DIM + V_DIM), KV_RANK ** -0.5),
        'w_out_even': nrm((N_EVEN, EVEN_MIX, D_MODEL), EVEN_MIX ** -0.5),
        'w_in_odd': nrm((N_ODD, D_MODEL, ODD_IN), D_MODEL ** -0.5),
        'w_out_odd': nrm((N_ODD, RET_HV, D_MODEL), RET_HV ** -0.5),
        'ffn_w_gate': nrm((DEPTH, D_MODEL, D_FF), D_MODEL ** -0.5),
        'ffn_w_up': nrm((DEPTH, D_MODEL, D_FF), D_MODEL ** -0.5),
        'ffn_w_down': nrm((DEPTH, D_FF, D_MODEL), D_FF ** -0.5),
    }


def reference(x_prompt, x_sample, cache_ckv, cache_kpe, state_conv, state_ret, page_table, meta_tokens,
              norm_mix, norm_ffn, norm_final, w_in_even, conv_w, conv_b, conv_ln_g, conv_ln_b,
              mla_q_norm, mla_w_uq, mla_kv_norm, mla_w_kv_up, w_out_even, w_in_odd, w_out_odd,
              ffn_w_gate, ffn_w_up, ffn_w_down):
    B, S_len, _ = x_prompt.shape
    T = x_sample.shape[1]
    meta = jnp.broadcast_to(meta_tokens[None].astype(x_prompt.dtype), (B, N_META, D_MODEL))
    h_p = jnp.concatenate([meta, x_prompt], 1)
    h_s = x_sample
    pos_p = jnp.arange(N_META + S_len)
    pos_s = PAST_LEN + jnp.arange(T)
    log_g = retention_log_decay()
    p_ckv, p_kpe, s_ckv, s_kpe, p_conv, s_conv, p_ret, s_ret = [], [], [], [], [], [], [], []
    for layer in range(DEPTH):
        if layer % 2 == 0:
            e = layer // 2
            wkv = mla_w_kv_up[e]
            att_p = functools.partial(mla_prompt_attend, w_kv_up=wkv, pos=pos_p)
            att_s = functools.partial(mla_sample_attend, w_kv_up=wkv, cache_ckv=cache_ckv, cache_kpe=cache_kpe,
                                      page_table=page_table, li=e)
            out, buf, ckv, kpe = even_mixer(rmsnorm(h_p, norm_mix[layer]), pos_p,
                                            jnp.zeros((B, CONV_WIDTH - 1, CONV_CH), h_p.dtype), att_p,
                                            w_in_even[e], conv_w[e], conv_b[e], conv_ln_g[e], conv_ln_b[e],
                                            mla_q_norm[e], mla_w_uq[e], mla_kv_norm[e], w_out_even[e])
            h_p = h_p + out.astype(h_p.dtype)
            p_conv.append(buf); p_ckv.append(ckv); p_kpe.append(kpe)
            out, buf, ckv, kpe = even_mixer(rmsnorm(h_s, norm_mix[layer]), pos_s, state_conv[e], att_s,
                                            w_in_even[e], conv_w[e], conv_b[e], conv_ln_g[e], conv_ln_b[e],
                                            mla_q_norm[e], mla_w_uq[e], mla_kv_norm[e], w_out_even[e])
            h_s = h_s + out.astype(h_s.dtype)
            s_conv.append(buf); s_ckv.append(ckv); s_kpe.append(kpe)
        else:
            o = layer // 2
            out, S = retention_mixer(rmsnorm(h_p, norm_mix[layer]), pos_p,
                                     jnp.zeros((B, RET_HEADS, RET_DK, RET_DV), F32), retention_prompt_run,
                                     log_g, w_in_odd[o], w_out_odd[o])
            h_p = h_p + out.astype(h_p.dtype)
            p_ret.append(S)
            out, S = retention_mixer(rmsnorm(h_s, norm_mix[layer]), pos_s, state_ret[o], retention_chunk,
                                     log_g, w_in_odd[o], w_out_odd[o])
            h_s = h_s + out.astype(h_s.dtype)
            s_ret.append(S)
        h_p = h_p + swiglu(rmsnorm(h_p, norm_ffn[layer]), ffn_w_gate[layer], ffn_w_up[layer], ffn_w_down[layer]).astype(h_p.dtype)
        h_s = h_s + swiglu(rmsnorm(h_s, norm_ffn[layer]), ffn_w_gate[layer], ffn_w_up[layer], ffn_w_down[layer]).astype(h_s.dtype)
    y_prompt = rmsnorm(h_p, norm_final)[:, N_META:]
    y_sample = rmsnorm(h_s, norm_final)
    return (y_prompt, y_sample, jnp.stack(p_ckv), jnp.stack(p_kpe), jnp.stack(s_ckv), jnp.stack(s_kpe),
            jnp.stack(p_conv), jnp.stack(s_conv), jnp.stack(p_ret), jnp.stack(s_ret))
```

```python
import functools

import jax
import jax.numpy as jnp
from jax import lax
from jax.experimental import pallas as pl
from jax.experimental.pallas import tpu as pltpu

F32 = jnp.float32
BF16 = jnp.bfloat16

D_MODEL = 2048
BATCH = 4
SEQ = 2048
DEPTH = 4
DEC_BATCH = 128
DEC_SEQ = 8
PAST_LEN = 8192
PAGE_SIZE = 128
N_META = 16
N_EVEN = 2
N_ODD = 2
EPS = 1e-6
ROPE_BASE = 10000.0
CONV_CH = 1024
CONV_WIDTH = 31
MLA_HEADS = 8
Q_RANK = 512
KV_RANK = 512
NOPE_DIM = 128
ROPE_DIM = 64
V_DIM = 128
QK_DIM = NOPE_DIM + ROPE_DIM
MLA_SCALE = QK_DIM ** -0.5
RET_HEADS = 8
RET_DK = D_MODEL // RET_HEADS
RET_DV = 2 * RET_DK
RET_CHUNK = 128
RET_HK = RET_HEADS * RET_DK
RET_HV = RET_HEADS * RET_DV
D_FF = -(-8 * D_MODEL // (3 * 256)) * 256

N_PROMPT_ROWS = BATCH * SEQ
N_META_ROWS = BATCH * N_META
N_SAMPLE_ROWS = DEC_BATCH * DEC_SEQ
ROW_META = N_PROMPT_ROWS
ROW_SAMPLE = N_PROMPT_ROWS + N_META_ROWS
N_ROWS = ROW_SAMPLE + N_SAMPLE_ROWS
N_PAGES = PAST_LEN // PAGE_SIZE

LANES = 128
HEAD_PAD = 2 * LANES
VMEM_LIMIT = 56 * 1024 * 1024
TM_WIDE = 928
TM_NARROW = 464
CONV_ROWS = 512
CONV_HALO = 32
CONV_CHUNK = 32
ATTN_BLOCK = 512
PAGES_PER_STEP = 16
RET_SEQS_PER_STEP = 4
CONV_SEQS_PER_STEP = 8


def _params(*semantics):
    return pltpu.CompilerParams(dimension_semantics=semantics, vmem_limit_bytes=VMEM_LIMIT)


def _dot(a, b):
    return jnp.dot(a, b, preferred_element_type=F32)


def _dot_nt(a, b):
    return lax.dot_general(a, b, (((1,), (1,)), ((), ())), preferred_element_type=F32)


def _dot_tn(a, b):
    return lax.dot_general(a, b, (((0,), (0,)), ((), ())), preferred_element_type=F32)


def _rms(x, gain):
    return x * lax.rsqrt(jnp.mean(x * x, -1, keepdims=True) + EPS) * gain


def _rmsnorm_kernel(x_ref, g_ref, o_ref):
    o_ref[...] = _rms(x_ref[...], g_ref[...]).astype(o_ref.dtype)


def _rmsnorm_rows(x, gain):
    tm = TM_WIDE
    return pl.pallas_call(
        _rmsnorm_kernel,
        out_shape=jax.ShapeDtypeStruct((N_ROWS, D_MODEL), BF16),
        grid=(N_ROWS // tm,),
        in_specs=[pl.BlockSpec((tm, D_MODEL), lambda i: (i, 0)),
                  pl.BlockSpec((1, D_MODEL), lambda i: (0, 0))],
        out_specs=pl.BlockSpec((tm, D_MODEL), lambda i: (i, 0)),
        compiler_params=_params("parallel"),
        name="rmsnorm_rows",
    )(x, gain.reshape(1, D_MODEL))


def _glu_kernel(x_ref, wa_ref, wg_ref, o_ref):
    x = x_ref[...]
    o_ref[...] = _dot(x, wa_ref[...]) * jax.nn.sigmoid(_dot(x, wg_ref[...]))


def _glu_proj(xn, w):
    tm, tn = TM_WIDE, 512
    nj = CONV_CH // tn
    return pl.pallas_call(
        _glu_kernel,
        out_shape=jax.ShapeDtypeStruct((N_ROWS, CONV_CH), F32),
        grid=(N_ROWS // tm, nj),
        in_specs=[pl.BlockSpec((tm, D_MODEL), lambda i, j: (i, 0)),
                  pl.BlockSpec((D_MODEL, tn), lambda i, j: (0, j)),
                  pl.BlockSpec((D_MODEL, tn), lambda i, j: (0, j + nj))],
        out_specs=pl.BlockSpec((tm, tn), lambda i, j: (i, j)),
        compiler_params=_params("parallel", "arbitrary"),
        name="glu_proj",
    )(xn, w, w)


def _latent_kernel(x_ref, wl_ref, qn_ref, kvn_ref, wq_ref, wqr_ref, wkv_ref, cos_ref, sin_ref,
                   q_ref, kv_ref, ckv_ref, kpe_ref, kpeb_ref):
    z = _dot(x_ref[...], wl_ref[...])
    cos = cos_ref[...]
    sin = sin_ref[...]
    cqn = _rms(z[:, :Q_RANK], qn_ref[...]).astype(BF16)
    ckvn = _rms(z[:, Q_RANK:Q_RANK + KV_RANK], kvn_ref[...])
    ckv_ref[...] = ckvn
    k0 = Q_RANK + KV_RANK
    kpe = z[:, k0:k0 + LANES] * cos + z[:, k0 + LANES:k0 + 2 * LANES] * sin
    kpe_ref[...] = kpe
    kpeb_ref[...] = kpe.astype(BF16)
    q = _dot(cqn, wq_ref[...])
    qr = _dot(cqn, wqr_ref[...])
    for h in range(MLA_HEADS):
        c0 = h * HEAD_PAD
        q_ref[:, c0:c0 + LANES] = q[:, c0:c0 + LANES].astype(BF16)
        pe = q[:, c0 + LANES:c0 + HEAD_PAD] * cos + qr[:, h * LANES:(h + 1) * LANES] * sin
        q_ref[:, c0 + LANES:c0 + HEAD_PAD] = pe.astype(BF16)
    kv_ref[...] = _dot(ckvn.astype(BF16), wkv_ref[...]).astype(BF16)


def _latent_proj(xn, wl, q_norm, kv_norm, wq, wqr, wkv, cos, sin):
    tm = TM_NARROW
    nl = wl.shape[1]
    full = lambda shape: pl.BlockSpec(shape, lambda i: (0, 0))
    rows = lambda n: pl.BlockSpec((tm, n), lambda i: (i, 0))
    return pl.pallas_call(
        _latent_kernel,
        out_shape=(jax.ShapeDtypeStruct((N_ROWS, MLA_HEADS * HEAD_PAD), BF16),
                   jax.ShapeDtypeStruct((N_ROWS, 2 * MLA_HEADS * LANES), BF16),
                   jax.ShapeDtypeStruct((N_ROWS, KV_RANK), F32),
                   jax.ShapeDtypeStruct((N_ROWS, LANES), F32),
                   jax.ShapeDtypeStruct((N_ROWS, LANES), BF16)),
        grid=(N_ROWS // tm,),
        in_specs=[rows(D_MODEL), full((D_MODEL, nl)), full((1, Q_RANK)), full((1, KV_RANK)),
                  full(wq.shape), full(wqr.shape), full(wkv.shape), rows(LANES), rows(LANES)],
        out_specs=(rows(MLA_HEADS * HEAD_PAD), rows(2 * MLA_HEADS * LANES), rows(KV_RANK),
                   rows(LANES), rows(LANES)),
        compiler_params=_params("parallel"),
        name="latent_proj",
    )(xn, wl, q_norm.reshape(1, Q_RANK), kv_norm.reshape(1, KV_RANK), wq, wqr, wkv, cos, sin)


def _rope_halves(t, cos, sin):
    half = RET_DK // 2
    outs = []
    for h in range(t.shape[1] // RET_DK):
        x1 = t[:, h * RET_DK:h * RET_DK + half]
        x2 = t[:, h * RET_DK + half:(h + 1) * RET_DK]
        outs += [x1 * cos - x2 * sin, x2 * cos + x1 * sin]
    return jnp.concatenate(outs, -1)


def _qk_kernel(x_ref, wq_ref, wk_ref, cos_ref, sin_ref, q_ref, k_ref):
    x = x_ref[...]
    cos = cos_ref[...]
    sin = sin_ref[...]
    q_ref[...] = _rope_halves(_dot(x, wq_ref[...]), cos, sin).astype(BF16)
    k_ref[...] = _rope_halves(_dot(x, wk_ref[...]), cos, sin) * (RET_DK ** -0.5)


def _ret_qk_proj(xn, w, cos, sin):
    tm, tn = TM_WIDE, 512
    nj = RET_HK // tn
    return pl.pallas_call(
        _qk_kernel,
        out_shape=(jax.ShapeDtypeStruct((N_ROWS, RET_HK), BF16),
                   jax.ShapeDtypeStruct((N_ROWS, RET_HK), F32)),
        grid=(N_ROWS // tm, nj),
        in_specs=[pl.BlockSpec((tm, D_MODEL), lambda i, j: (i, 0)),
                  pl.BlockSpec((D_MODEL, tn), lambda i, j: (0, j)),
                  pl.BlockSpec((D_MODEL, tn), lambda i, j: (0, j + nj)),
                  pl.BlockSpec((tm, RET_DK // 2), lambda i, j: (i, 0)),
                  pl.BlockSpec((tm, RET_DK // 2), lambda i, j: (i, 0))],
        out_specs=(pl.BlockSpec((tm, tn), lambda i, j: (i, j)),
                   pl.BlockSpec((tm, tn), lambda i, j: (i, j))),
        compiler_params=_params("parallel", "arbitrary"),
        name="ret_qk_proj",
    )(xn, w, w, cos, sin)


def _vg_kernel(x_ref, wv_ref, wg_ref, v_ref, g_ref):
    x = x_ref[...]
    v_ref[...] = _dot(x, wv_ref[...]).astype(BF16)
    g_ref[...] = _dot(x, wg_ref[...])


def _ret_vg_proj(xn, w):
    tm, tn = TM_WIDE, 512
    nj = RET_HV // tn
    off = 2 * RET_HK // tn
    return pl.pallas_call(
        _vg_kernel,
        out_shape=(jax.ShapeDtypeStruct((N_ROWS, RET_HV), BF16),
                   jax.ShapeDtypeStruct((N_ROWS, RET_HV), F32)),
        grid=(N_ROWS // tm, nj),
        in_specs=[pl.BlockSpec((tm, D_MODEL), lambda i, j: (i, 0)),
                  pl.BlockSpec((D_MODEL, tn), lambda i, j: (0, off + j)),
                  pl.BlockSpec((D_MODEL, tn), lambda i, j: (0, off + nj + j))],
        out_specs=(pl.BlockSpec((tm, tn), lambda i, j: (i, j)),
                   pl.BlockSpec((tm, tn), lambda i, j: (i, j))),
        compiler_params=_params("parallel", "arbitrary"),
        name="ret_vg_proj",
    )(xn, w, w)


def _ffn_up_kernel(x_ref, wg_ref, wu_ref, o_ref):
    x = x_ref[...]
    o_ref[...] = (jax.nn.silu(_dot(x, wg_ref[...])) * _dot(x, wu_ref[...])).astype(BF16)


def _ffn_up(xn, wg, wu):
    tm, tn = TM_WIDE, 512
    return pl.pallas_call(
        _ffn_up_kernel,
        out_shape=jax.ShapeDtypeStruct((N_ROWS, D_FF), BF16),
        grid=(N_ROWS // tm, D_FF // tn),
        in_specs=[pl.BlockSpec((tm, D_MODEL), lambda i, j: (i, 0)),
                  pl.BlockSpec((D_MODEL, tn), lambda i, j: (0, j)),
                  pl.BlockSpec((D_MODEL, tn), lambda i, j: (0, j))],
        out_specs=pl.BlockSpec((tm, tn), lambda i, j: (i, j)),
        compiler_params=_params("parallel", "arbitrary"),
        name="ffn_up",
    )(xn, wg, wu)


def _out_proj_kernel(n_lhs, nk, *refs):
    a_refs = refs[:n_lhs]
    w_refs = refs[n_lhs:2 * n_lhs]
    h_ref, g_ref, ho_ref, no_ref = refs[2 * n_lhs:]
    k = pl.program_id(1)
    part = _dot(a_refs[0][...], w_refs[0][...])
    for a_ref, w_ref in zip(a_refs[1:], w_refs[1:]):
        part = part + _dot(a_ref[...], w_ref[...])

    @pl.when(k == 0)
    def _():
        ho_ref[...] = h_ref[...] + part

    @pl.when(k > 0)
    def _():
        ho_ref[...] += part

    @pl.when(k == nk - 1)
    def _():
        no_ref[...] = _rms(ho_ref[...], g_ref[...]).astype(no_ref.dtype)


def _out_proj(lhs, ws, h, gain, nk, norm_dtype):
    tm = TM_NARROW
    n = len(lhs)
    in_specs = []
    for a in lhs:
        in_specs.append(pl.BlockSpec((tm, a.shape[1] // nk), lambda i, k: (i, k)))
    for w in ws:
        in_specs.append(pl.BlockSpec((w.shape[0] // nk, D_MODEL), lambda i, k: (k, 0)))
    in_specs += [pl.BlockSpec((tm, D_MODEL), lambda i, k: (i, 0)),
                 pl.BlockSpec((1, D_MODEL), lambda i, k: (0, 0))]
    return pl.pallas_call(
        functools.partial(_out_proj_kernel, n, nk),
        out_shape=(jax.ShapeDtypeStruct((N_ROWS, D_MODEL), F32),
                   jax.ShapeDtypeStruct((N_ROWS, D_MODEL), norm_dtype)),
        grid=(N_ROWS // tm, nk),
        in_specs=in_specs,
        out_specs=(pl.BlockSpec((tm, D_MODEL), lambda i, k: (i, 0)),
                   pl.BlockSpec((tm, D_MODEL), lambda i, k: (i, 0))),
        compiler_params=_params("parallel", "arbitrary"),
        name="out_proj",
    )(*lhs, *ws, h, gain.reshape(1, D_MODEL))


def _conv_taps(win, w_ref, lane0, rows):
    shift0 = CONV_HALO - (CONV_WIDTH - 1)
    acc = None
    for s in range(8):
        n_a = (CONV_WIDTH - 1 - s) // 8 + 1
        shifted = win[shift0 + s:shift0 + s + rows + 8 * (n_a - 1), :]
        for a in range(n_a):
            w = 8 * a + s
            term = shifted[8 * a:8 * a + rows, :] * w_ref[w:w + 1, lane0:lane0 + LANES]
            acc = term if acc is None else acc + term
    return acc


def _conv_post(y, b_ref, g_ref, beta_ref):
    y = y + b_ref[...]
    mu = jnp.mean(y, -1, keepdims=True)
    yc = y - mu
    var = jnp.mean(yc * yc, -1, keepdims=True)
    return jax.nn.silu(yc * lax.rsqrt(var + EPS) * g_ref[...] + beta_ref[...])


def _conv_prompt_kernel(cur_ref, prev_ref, meta_ref, w_ref, b_ref, g_ref, beta_ref,
                        o_ref, ometa_ref, full_ref):
    t = pl.program_id(1)
    full_ref[CONV_HALO:, :] = cur_ref[...]

    @pl.when(t == 0)
    def _():
        full_ref[0:CONV_HALO - N_META, :] = jnp.zeros((CONV_HALO - N_META, CONV_CH), F32)
        full_ref[CONV_HALO - N_META:CONV_HALO, :] = meta_ref[...]
        mwin = jnp.concatenate([jnp.zeros((CONV_HALO, CONV_CH), F32), meta_ref[...]], 0)
        parts = [_conv_taps(mwin[:, l * LANES:(l + 1) * LANES], w_ref, l * LANES, N_META)
                 for l in range(CONV_CH // LANES)]
        ometa_ref[...] = _conv_post(jnp.concatenate(parts, -1), b_ref, g_ref, beta_ref).astype(BF16)

    @pl.when(t > 0)
    def _():
        full_ref[0:CONV_HALO, :] = prev_ref[...]

    def chunk(c, carry):
        base = pl.multiple_of(c * CONV_CHUNK, CONV_CHUNK)
        parts = []
        for l in range(CONV_CH // LANES):
            win = full_ref[pl.ds(base, CONV_CHUNK + CONV_HALO), l * LANES:(l + 1) * LANES]
            parts.append(_conv_taps(win, w_ref, l * LANES, CONV_CHUNK))
        y = _conv_post(jnp.concatenate(parts, -1), b_ref, g_ref, beta_ref)
        o_ref[pl.ds(base, CONV_CHUNK), :] = y.astype(BF16)
        return carry

    lax.fori_loop(0, CONV_ROWS // CONV_CHUNK, chunk, 0)


def _conv_prompt(u, conv_w, conv_b, ln_g, ln_b):
    nt = SEQ // CONV_ROWS
    vec = lambda: pl.BlockSpec((1, CONV_CH), lambda b, t: (0, 0))
    return pl.pallas_call(
        _conv_prompt_kernel,
        out_shape=(jax.ShapeDtypeStruct((N_PROMPT_ROWS, CONV_CH), BF16),
                   jax.ShapeDtypeStruct((N_META_ROWS, CONV_CH), BF16)),
        grid=(BATCH, nt),
        in_specs=[pl.BlockSpec((CONV_ROWS, CONV_CH), lambda b, t: (b * nt + t, 0)),
                  pl.BlockSpec((CONV_HALO, CONV_CH),
                               lambda b, t: (jnp.maximum((b * SEQ + t * CONV_ROWS) // CONV_HALO - 1, 0), 0)),
                  pl.BlockSpec((N_META, CONV_CH), lambda b, t: (ROW_META // N_META + b, 0)),
                  pl.BlockSpec((CONV_WIDTH, CONV_CH), lambda b, t: (0, 0)),
                  vec(), vec(), vec()],
        out_specs=(pl.BlockSpec((CONV_ROWS, CONV_CH), lambda b, t: (b * nt + t, 0)),
                   pl.BlockSpec((N_META, CONV_CH), lambda b, t: (b, 0))),
        scratch_shapes=[pltpu.VMEM((CONV_ROWS + CONV_HALO, CONV_CH), F32)],
        compiler_params=_params("parallel", "arbitrary"),
        name="conv_prompt",
    )(u, u, u, conv_w, conv_b.reshape(1, CONV_CH), ln_g.reshape(1, CONV_CH), ln_b.reshape(1, CONV_CH))


def _conv_sample_kernel(u_ref, st_ref, w_ref, b_ref, g_ref, beta_ref, o_ref, ns_ref, full_ref):
    keep = CONV_WIDTH - 1
    shift0 = CONV_HALO - keep
    for s in range(CONV_SEQS_PER_STEP):
        u = u_ref[s]
        full_ref[0:8, :] = jnp.zeros((8, CONV_CH), F32)
        full_ref[shift0:CONV_HALO, :] = st_ref[s]
        full_ref[CONV_HALO:, :] = u
        parts = [_conv_taps(full_ref[:, l * LANES:(l + 1) * LANES], w_ref, l * LANES, DEC_SEQ)
                 for l in range(CONV_CH // LANES)]
        o_ref[s] = _conv_post(jnp.concatenate(parts, -1), b_ref, g_ref, beta_ref)
        ns_ref[s, 0:keep - DEC_SEQ, :] = st_ref[s, DEC_SEQ:keep, :]
        ns_ref[s, keep - DEC_SEQ:keep, :] = u


def _conv_sample(u_s, state, conv_w, conv_b, ln_g, ln_b):
    g = CONV_SEQS_PER_STEP
    keep = CONV_WIDTH - 1
    vec = lambda: pl.BlockSpec((1, CONV_CH), lambda i: (0, 0))
    return pl.pallas_call(
        _conv_sample_kernel,
        out_shape=(jax.ShapeDtypeStruct((DEC_BATCH, DEC_SEQ, CONV_CH), F32),
                   jax.ShapeDtypeStruct((DEC_BATCH, keep, CONV_CH), F32)),
        grid=(DEC_BATCH // g,),
        in_specs=[pl.BlockSpec((g, DEC_SEQ, CONV_CH), lambda i: (i, 0, 0)),
                  pl.BlockSpec((g, keep, CONV_CH), lambda i: (i, 0, 0)),
                  pl.BlockSpec((CONV_WIDTH, CONV_CH), lambda i: (0, 0)),
                  vec(), vec(), vec()],
        out_specs=(pl.BlockSpec((g, DEC_SEQ, CONV_CH), lambda i: (i, 0, 0)),
                   pl.BlockSpec((g, keep, CONV_CH), lambda i: (i, 0, 0))),
        scratch_shapes=[pltpu.VMEM((CONV_HALO + DEC_SEQ, CONV_CH), F32)],
        compiler_params=_params("parallel"),
        name="conv_sample",
    )(u_s, state, conv_w, conv_b.reshape(1, CONV_CH), ln_g.reshape(1, CONV_CH), ln_b.reshape(1, CONV_CH))


def _softmax_step(q, kc, vc, m, l, acc, mask=None):
    s = _dot_nt(q, kc) * MLA_SCALE
    if mask is not None:
        s = jnp.where(mask, s, -jnp.inf)
    m_new = jnp.maximum(m, jnp.max(s, -1, keepdims=True))
    alpha = jnp.exp(m - m_new)
    p = jnp.exp(s - m_new)
    l = alpha * l + jnp.sum(p, -1, keepdims=True)
    acc = alpha * acc + _dot(p.astype(BF16), vc)
    return m_new, l, acc


def _causal_mask(n):
    return lax.broadcasted_iota(jnp.int32, (n, n), 0) >= lax.broadcasted_iota(jnp.int32, (n, n), 1)


def _attn_prompt_kernel(q_ref, qm_ref, kn_ref, kp_ref, v_ref, knm_ref, kpm_ref, vm_ref, o_ref, om_ref):
    qi = pl.program_id(2)
    blk = ATTN_BLOCK
    k_meta = jnp.concatenate([knm_ref[...], kpm_ref[...]], -1)
    v_meta = vm_ref[...]

    @pl.when(qi == 0)
    def _():
        init = (jnp.full((N_META, 1), -jnp.inf, F32), jnp.zeros((N_META, 1), F32),
                jnp.zeros((N_META, V_DIM), F32))
        _, l, acc = _softmax_step(qm_ref[...], k_meta, v_meta, *init, mask=_causal_mask(N_META))
        om_ref[...] = (acc / l).astype(BF16)

    q = q_ref[...]
    init = (jnp.full((blk, 1), -jnp.inf, F32), jnp.zeros((blk, 1), F32), jnp.zeros((blk, V_DIM), F32))
    carry = _softmax_step(q, k_meta, v_meta, *init)

    def keys(j):
        r0 = pl.multiple_of(j * blk, blk)
        kc = jnp.concatenate([kn_ref[pl.ds(r0, blk), :], kp_ref[pl.ds(r0, blk), :]], -1)
        return kc, v_ref[pl.ds(r0, blk), :]

    def body(j, c):
        kc, vc = keys(j)
        return _softmax_step(q, kc, vc, *c)

    carry = lax.fori_loop(0, qi, body, carry)
    kc, vc = keys(qi)
    _, l, acc = _softmax_step(q, kc, vc, *carry, mask=_causal_mask(blk))
    o_ref[...] = (acc / l).astype(BF16)


def _attn_prompt(q, kv, kpe):
    nq = SEQ // ATTN_BLOCK
    mrow = ROW_META // N_META
    return pl.pallas_call(
        _attn_prompt_kernel,
        out_shape=(jax.ShapeDtypeStruct((N_PROMPT_ROWS, MLA_HEADS * V_DIM), BF16),
                   jax.ShapeDtypeStruct((N_META_ROWS, MLA_HEADS * V_DIM), BF16)),
        grid=(BATCH, MLA_HEADS, nq),
        in_specs=[pl.BlockSpec((ATTN_BLOCK, HEAD_PAD), lambda b, h, i: (b * nq + i, h)),
                  pl.BlockSpec((N_META, HEAD_PAD), lambda b, h, i: (mrow + b, h)),
                  pl.BlockSpec((SEQ, LANES), lambda b, h, i: (b, h)),
                  pl.BlockSpec((SEQ, LANES), lambda b, h, i: (b, 0)),
                  pl.BlockSpec((SEQ, LANES), lambda b, h, i: (b, MLA_HEADS + h)),
                  pl.BlockSpec((N_META, LANES), lambda b, h, i: (mrow + b, h)),
                  pl.BlockSpec((N_META, LANES), lambda b, h, i: (mrow + b, 0)),
                  pl.BlockSpec((N_META, LANES), lambda b, h, i: (mrow + b, MLA_HEADS + h))],
        out_specs=(pl.BlockSpec((ATTN_BLOCK, V_DIM), lambda b, h, i: (b * nq + i, h)),
                   pl.BlockSpec((N_META, V_DIM), lambda b, h, i: (b, h))),
        compiler_params=_params("parallel", "parallel", "arbitrary"),
        name="attn_prompt",
    )(q, q, kv, kpe, kv, kv, kpe, kv)


def _absorb_q_kernel(q_ref, wuk_ref, qlat_ref, qpe_ref):
    q = q_ref[...]
    qlat = _dot(q[:, :NOPE_DIM], wuk_ref[...])
    qlat_ref[...] = qlat.reshape(DEC_BATCH, DEC_SEQ, KV_RANK)
    qpe_ref[...] = q[:, NOPE_DIM:].astype(F32).reshape(DEC_BATCH, DEC_SEQ, LANES)


def _absorb_q(q_s, wuk_t):
    rows = MLA_HEADS * DEC_SEQ
    return pl.pallas_call(
        _absorb_q_kernel,
        out_shape=(jax.ShapeDtypeStruct((DEC_BATCH, rows, KV_RANK), F32),
                   jax.ShapeDtypeStruct((DEC_BATCH, rows, LANES), F32)),
        grid=(MLA_HEADS,),
        in_specs=[pl.BlockSpec((N_SAMPLE_ROWS, HEAD_PAD), lambda h: (0, h)),
                  pl.BlockSpec((None, NOPE_DIM, KV_RANK), lambda h: (h, 0, 0))],
        out_specs=(pl.BlockSpec((DEC_BATCH, DEC_SEQ, KV_RANK), lambda h: (0, h, 0)),
                   pl.BlockSpec((DEC_BATCH, DEC_SEQ, LANES), lambda h: (0, h, 0))),
        compiler_params=_params("parallel"),
        name="absorb_q",
    )(q_s, wuk_t)


def _decode_kernel(pt_ref, qlat_ref, qpe_ref, *refs):
    g = PAGES_PER_STEP
    c_refs = refs[:g]
    k_refs = refs[g:2 * g]
    cn_ref, kn_ref, o_ref, m_ref, l_ref, acc_ref = refs[2 * g:]
    p_id = pl.program_id(1)
    rows = MLA_HEADS * DEC_SEQ

    @pl.when(p_id == 0)
    def _():
        m_ref[...] = jnp.full((rows, 1), -jnp.inf, F32)
        l_ref[...] = jnp.zeros((rows, 1), F32)
        acc_ref[...] = jnp.zeros((rows, KV_RANK), F32)

    ql = qlat_ref[...].astype(BF16)
    qp = qpe_ref[...][:, :ROPE_DIM].astype(BF16)
    cs = [c_ref[...].astype(BF16) for c_ref in c_refs]
    s = jnp.concatenate(
        [_dot_nt(ql, c) + _dot_nt(qp, k_ref[...].astype(BF16)) for c, k_ref in zip(cs, k_refs)],
        -1) * MLA_SCALE
    m_old = m_ref[...]
    m_new = jnp.maximum(m_old, jnp.max(s, -1, keepdims=True))
    alpha = jnp.exp(m_old - m_new)
    p = jnp.exp(s - m_new)
    l_ref[...] = alpha * l_ref[...] + jnp.sum(p, -1, keepdims=True)
    pv = _dot(p[:, :PAGE_SIZE].astype(BF16), cs[0])
    for i in range(1, g):
        pv = pv + _dot(p[:, i * PAGE_SIZE:(i + 1) * PAGE_SIZE].astype(BF16), cs[i])
    acc_ref[...] = alpha * acc_ref[...] + pv
    m_ref[...] = m_new

    @pl.when(p_id == pl.num_programs(1) - 1)
    def _():
        cn = cn_ref[...].astype(BF16)
        kn = kn_ref[...][:, :ROPE_DIM].astype(BF16)
        sn = (_dot_nt(ql, cn) + _dot_nt(qp, kn)) * MLA_SCALE
        tok = lax.broadcasted_iota(jnp.int32, (rows, DEC_SEQ), 0) % DEC_SEQ
        col = lax.broadcasted_iota(jnp.int32, (rows, DEC_SEQ), 1)
        sn = jnp.where(col <= tok, sn, -jnp.inf)
        m_prev = m_ref[...]
        m_fin = jnp.maximum(m_prev, jnp.max(sn, -1, keepdims=True))
        a = jnp.exp(m_prev - m_fin)
        pn = jnp.exp(sn - m_fin)
        l_fin = a * l_ref[...] + jnp.sum(pn, -1, keepdims=True)
        acc = a * acc_ref[...] + _dot(pn.astype(BF16), cn)
        o_ref[...] = acc / l_fin


def _decode_attn(page_table, qlat, qpe, cache_ckv, cache_kpe, li, ckv_rows, kpe_rows):
    g = PAGES_PER_STEP
    rows = MLA_HEADS * DEC_SEQ
    srow = ROW_SAMPLE // DEC_SEQ

    def page_spec(width, j):
        return pl.BlockSpec((None, None, PAGE_SIZE, width),
                            lambda b, p, pt: (li, pt[b * N_PAGES + p * g + j], 0, 0))

    in_specs = [pl.BlockSpec((None, rows, KV_RANK), lambda b, p, pt: (b, 0, 0)),
                pl.BlockSpec((None, rows, LANES), lambda b, p, pt: (b, 0, 0))]
    in_specs += [page_spec(KV_RANK, j) for j in range(g)]
    in_specs += [page_spec(ROPE_DIM, j) for j in range(g)]
    in_specs += [pl.BlockSpec((DEC_SEQ, KV_RANK), lambda b, p, pt: (srow + b, 0)),
                 pl.BlockSpec((DEC_SEQ, LANES), lambda b, p, pt: (srow + b, 0))]
    grid_spec = pltpu.PrefetchScalarGridSpec(
        num_scalar_prefetch=1,
        grid=(DEC_BATCH, N_PAGES // g),
        in_specs=in_specs,
        out_specs=pl.BlockSpec((None, rows, KV_RANK), lambda b, p, pt: (b, 0, 0)),
        scratch_shapes=[pltpu.VMEM((rows, 1), F32), pltpu.VMEM((rows, 1), F32),
                        pltpu.VMEM((rows, KV_RANK), F32)],
    )
    return pl.pallas_call(
        _decode_kernel,
        out_shape=jax.ShapeDtypeStruct((DEC_BATCH, rows, KV_RANK), F32),
        grid_spec=grid_spec,
        compiler_params=_params("parallel", "arbitrary"),
        name="decode_attn",
    )(page_table.reshape(-1), qlat, qpe, *([cache_ckv] * g), *([cache_kpe] * g), ckv_rows, kpe_rows)


def _v_up_kernel(o_ref, w_ref, a_ref):
    o = o_ref[...].reshape(N_SAMPLE_ROWS, KV_RANK).astype(BF16)
    a_ref[...] = _dot(o, w_ref[...]).astype(BF16)


def _v_up(o_lat, wuv):
    return pl.pallas_call(
        _v_up_kernel,
        out_shape=jax.ShapeDtypeStruct((N_SAMPLE_ROWS, MLA_HEADS * V_DIM), BF16),
        grid=(MLA_HEADS,),
        in_specs=[pl.BlockSpec((DEC_BATCH, DEC_SEQ, KV_RANK), lambda h: (0, h, 0)),
                  pl.BlockSpec((None, KV_RANK, V_DIM), lambda h: (h, 0, 0))],
        out_specs=pl.BlockSpec((N_SAMPLE_ROWS, V_DIM), lambda h: (0, h)),
        compiler_params=_params("parallel"),
        name="v_up",
    )(o_lat, wuv)


def _ret_tables(n, lg):
    i = lax.broadcasted_iota(jnp.int32, (n, 1), 0).astype(F32)
    rel = (lax.broadcasted_iota(jnp.int32, (n, n), 0) - lax.broadcasted_iota(jnp.int32, (n, n), 1)).astype(F32)
    decay = jnp.where(rel >= 0, jnp.exp(lg * jnp.maximum(rel, 0.0)), 0.0)
    q_scale = jnp.exp((i + 1.0) * lg)
    k_scale = jnp.exp((n - 1.0 - i) * lg)
    return decay, q_scale, k_scale


def _ret_chunk(q, k, v, g, s_old, tables, s_decay):
    decay, q_scale, k_scale = tables
    inner = _dot_nt(q, k.astype(BF16)) * decay
    o = _dot(inner.astype(BF16), v) + _dot(q, s_old.astype(BF16)) * q_scale
    s_new = s_decay * s_old + _dot_tn((k * k_scale).astype(BF16), v)
    mu = jnp.mean(o, -1, keepdims=True)
    oc = o - mu
    var = jnp.mean(oc * oc, -1, keepdims=True)
    y = jax.nn.silu(g) * (oc * lax.rsqrt(var + EPS))
    return y, s_new


def _ret_prompt_kernel(c_ref, q_ref, k_ref, v_ref, g_ref, qm_ref, km_ref, vm_ref, gm_ref,
                       y_ref, ym_ref, so_ref, s_ref):
    h = pl.program_id(1)
    lg = c_ref[h, 0]
    ym, s0 = _ret_chunk(qm_ref[...], km_ref[...], vm_ref[...], gm_ref[...],
                        jnp.zeros((RET_DK, RET_DV), F32), _ret_tables(N_META, lg), c_ref[h, 2])
    ym_ref[...] = ym.astype(BF16)
    s_ref[...] = s0
    tables = _ret_tables(RET_CHUNK, lg)
    s_decay = c_ref[h, 1]

    def body(c, carry):
        r0 = pl.multiple_of(c * RET_CHUNK, RET_CHUNK)
        rows = pl.ds(r0, RET_CHUNK)
        y, s_new = _ret_chunk(q_ref[rows, :], k_ref[rows, :], v_ref[rows, :], g_ref[rows, :],
                              s_ref[...], tables, s_decay)
        y_ref[rows, :] = y.astype(BF16)
        s_ref[...] = s_new
        return carry

    lax.fori_loop(0, SEQ // RET_CHUNK, body, 0)
    so_ref[...] = s_ref[...]


def _ret_prompt(consts, q, k, v, g):
    mrow = ROW_META // N_META
    seq = lambda w: pl.BlockSpec((SEQ, w), lambda b, h: (b, h))
    meta = lambda w: pl.BlockSpec((N_META, w), lambda b, h: (mrow + b, h))
    return pl.pallas_call(
        _ret_prompt_kernel,
        out_shape=(jax.ShapeDtypeStruct((N_PROMPT_ROWS, RET_HV), BF16),
                   jax.ShapeDtypeStruct((N_META_ROWS, RET_HV), BF16),
                   jax.ShapeDtypeStruct((BATCH, RET_HEADS, RET_DK, RET_DV), F32)),
        grid=(BATCH, RET_HEADS),
        in_specs=[pl.BlockSpec(memory_space=pltpu.SMEM),
                  seq(RET_DK), seq(RET_DK), seq(RET_DV), seq(RET_DV),
                  meta(RET_DK), meta(RET_DK), meta(RET_DV), meta(RET_DV)],
        out_specs=(seq(RET_DV), pl.BlockSpec((N_META, RET_DV), lambda b, h: (b, h)),
                   pl.BlockSpec((None, None, RET_DK, RET_DV), lambda b, h: (b, h, 0, 0))),
        scratch_shapes=[pltpu.VMEM((RET_DK, RET_DV), F32)],
        compiler_params=_params("parallel", "parallel"),
        name="ret_prompt",
    )(consts, q, k, v, g, q, k, v, g)


def _ret_sample_kernel(c_ref, q_ref, k_ref, v_ref, g_ref, s_ref, y_ref, so_ref):
    h = pl.program_id(1)
    tables = _ret_tables(DEC_SEQ, c_ref[h, 0])
    s_decay = c_ref[h, 3]
    q_all = q_ref[...].astype(F32)
    k_all = k_ref[...]
    v_all = v_ref[...].astype(F32)
    g_all = g_ref[...]
    ys = []
    for s in range(RET_SEQS_PER_STEP):
        rows = slice(s * DEC_SEQ, (s + 1) * DEC_SEQ)
        y, s_new = _ret_chunk(q_all[rows].astype(BF16), k_all[rows], v_all[rows].astype(BF16), g_all[rows],
                              s_ref[s], tables, s_decay)
        ys.append(y)
        so_ref[s] = s_new
    y_ref[...] = jnp.concatenate(ys, 0).astype(BF16)


def _ret_sample(consts, q, k, v, g, state):
    gs = RET_SEQS_PER_STEP
    rows = gs * DEC_SEQ
    r0 = ROW_SAMPLE // rows
    tok = lambda w: pl.BlockSpec((rows, w), lambda i, h: (r0 + i, h))
    st = pl.BlockSpec((gs, None, RET_DK, RET_DV), lambda i, h: (i, h, 0, 0))
    return pl.pallas_call(
        _ret_sample_kernel,
        out_shape=(jax.ShapeDtypeStruct((N_SAMPLE_ROWS, RET_HV), BF16),
                   jax.ShapeDtypeStruct((DEC_BATCH, RET_HEADS, RET_DK, RET_DV), F32)),
        grid=(DEC_BATCH // gs, RET_HEADS),
        in_specs=[pl.BlockSpec(memory_space=pltpu.SMEM),
                  tok(RET_DK), tok(RET_DK), tok(RET_DV), tok(RET_DV), st],
        out_specs=(pl.BlockSpec((rows, RET_DV), lambda i, h: (i, h)), st),
        compiler_params=_params("parallel", "parallel"),
        name="ret_sample",
    )(consts, q, k, v, g, state)


def _rotate_half_cols(w):
    half = w.shape[-1] // 2
    return jnp.concatenate([-w[..., half:], w[..., :half]], -1)


def _pad_cols(w, n):
    return jnp.pad(w, [(0, 0)] * (w.ndim - 1) + [(0, n - w.shape[-1])])


def _even_weights(w_in, w_uq, w_kv_up):
    conv_cols = 2 * CONV_CH
    w_glu = w_in[:, :conv_cols].astype(BF16)
    w_kpe = w_in[:, conv_cols + Q_RANK + KV_RANK:]
    w_lat = jnp.concatenate([w_in[:, conv_cols:conv_cols + Q_RANK + KV_RANK],
                             _pad_cols(w_kpe, LANES), _pad_cols(_rotate_half_cols(w_kpe), LANES)], -1).astype(BF16)
    w_q = _pad_cols(w_uq, HEAD_PAD).reshape(Q_RANK, MLA_HEADS * HEAD_PAD).astype(BF16)
    w_qr = _pad_cols(_rotate_half_cols(w_uq[..., NOPE_DIM:]), LANES).reshape(Q_RANK, MLA_HEADS * LANES).astype(BF16)
    w_uk = w_kv_up[..., :NOPE_DIM]
    w_uv = w_kv_up[..., NOPE_DIM:]
    w_kv = jnp.concatenate([w_uk.reshape(KV_RANK, -1), w_uv.reshape(KV_RANK, -1)], -1).astype(BF16)
    w_uk_t = jnp.transpose(w_uk, (1, 2, 0)).astype(BF16)
    w_uv_h = jnp.transpose(w_uv, (1, 0, 2)).astype(BF16)
    return w_glu, w_lat, w_q, w_qr, w_kv, w_uk_t, w_uv_h


def _rope_tables(pos, d):
    inv = ROPE_BASE ** (-jnp.arange(d // 2, dtype=F32) * 2.0 / d)
    ang = pos.astype(F32)[:, None] * inv
    return jnp.cos(ang), jnp.sin(ang)


def _ret_consts():
    lg = jnp.log1p(-(2.0 ** (-5.0 - jnp.arange(RET_HEADS, dtype=F32))))
    return jnp.stack([lg, jnp.exp(RET_CHUNK * lg), jnp.exp(N_META * lg), jnp.exp(DEC_SEQ * lg)], -1)


def kernel(x_prompt, x_sample, cache_ckv, cache_kpe, state_conv, state_ret, page_table, meta_tokens,
           norm_mix, norm_ffn, norm_final, w_in_even, conv_w, conv_b, conv_ln_g, conv_ln_b,
           mla_q_norm, mla_w_uq, mla_kv_norm, mla_w_kv_up, w_out_even, w_in_odd, w_out_odd,
           ffn_w_gate, ffn_w_up, ffn_w_down):
    h = jnp.concatenate([x_prompt.reshape(N_PROMPT_ROWS, D_MODEL),
                         jnp.tile(meta_tokens.astype(x_prompt.dtype), (BATCH, 1)),
                         x_sample.reshape(N_SAMPLE_ROWS, D_MODEL)], 0)
    pos = jnp.concatenate([jnp.tile(N_META + jnp.arange(SEQ), BATCH),
                           jnp.tile(jnp.arange(N_META), BATCH),
                           jnp.tile(PAST_LEN + jnp.arange(DEC_SEQ), DEC_BATCH)])
    cos_m, sin_m = _rope_tables(pos, ROPE_DIM)
    cos_mla = _pad_cols(jnp.concatenate([cos_m, cos_m], -1), LANES)
    sin_mla = _pad_cols(jnp.concatenate([sin_m, sin_m], -1), LANES)
    cos_ret, sin_ret = _rope_tables(pos, RET_DK)
    ret_consts = _ret_consts()

    p_ckv, p_kpe, s_ckv, s_kpe, p_conv, s_conv, p_ret, s_ret = [], [], [], [], [], [], [], []
    xn = _rmsnorm_rows(h, norm_mix[0])
    for layer in range(DEPTH):
        if layer % 2 == 0:
            e = layer // 2
            w_glu, w_lat, w_q, w_qr, w_kv, w_uk_t, w_uv_h = _even_weights(w_in_even[e], mla_w_uq[e], mla_w_kv_up[e])
            u = _glu_proj(xn, w_glu)
            q, kv, ckv, kpe, kpe_b = _latent_proj(xn, w_lat, mla_q_norm[e], mla_kv_norm[e],
                                                  w_q, w_qr, w_kv, cos_mla, sin_mla)
            co_p, co_m = _conv_prompt(u, conv_w[e], conv_b[e], conv_ln_g[e], conv_ln_b[e])
            co_s, new_state = _conv_sample(u[ROW_SAMPLE:].reshape(DEC_BATCH, DEC_SEQ, CONV_CH), state_conv[e],
                                           conv_w[e], conv_b[e], conv_ln_g[e], conv_ln_b[e])
            at_p, at_m = _attn_prompt(q, kv, kpe_b)
            qlat, qpe = _absorb_q(q[ROW_SAMPLE:], w_uk_t)
            o_lat = _decode_attn(page_table, qlat, qpe, cache_ckv, cache_kpe, e, ckv, kpe)
            at_s = _v_up(o_lat, w_uv_h)
            conv_out = jnp.concatenate([co_p, co_m, co_s.reshape(N_SAMPLE_ROWS, CONV_CH).astype(BF16)], 0)
            attn = jnp.concatenate([at_p, at_m, at_s], 0)
            w_out = w_out_even[e].astype(BF16)
            h, xn = _out_proj([conv_out, attn], [w_out[:CONV_CH], w_out[CONV_CH:]], h, norm_ffn[layer], 1, BF16)
            u_p = u[:N_PROMPT_ROWS].reshape(BATCH, SEQ, CONV_CH)
            p_conv.append(u_p[:, SEQ - (CONV_WIDTH - 1):])
            s_conv.append(new_state)
            ckv_p = ckv[:N_PROMPT_ROWS].reshape(BATCH, SEQ, KV_RANK)
            ckv_m = ckv[ROW_META:ROW_SAMPLE].reshape(BATCH, N_META, KV_RANK)
            p_ckv.append(jnp.concatenate([ckv_m, ckv_p], 1))
            kpe_p = kpe[:N_PROMPT_ROWS, :ROPE_DIM].reshape(BATCH, SEQ, ROPE_DIM)
            kpe_m = kpe[ROW_META:ROW_SAMPLE, :ROPE_DIM].reshape(BATCH, N_META, ROPE_DIM)
            p_kpe.append(jnp.concatenate([kpe_m, kpe_p], 1))
            s_ckv.append(ckv[ROW_SAMPLE:].reshape(DEC_BATCH, DEC_SEQ, KV_RANK))
            s_kpe.append(kpe[ROW_SAMPLE:, :ROPE_DIM].reshape(DEC_BATCH, DEC_SEQ, ROPE_DIM))
        else:
            o = layer // 2
            w_in = w_in_odd[o].astype(BF16)
            q, k = _ret_qk_proj(xn, w_in, cos_ret, sin_ret)
            v, g = _ret_vg_proj(xn, w_in)
            y_p, y_m, st_p = _ret_prompt(ret_consts, q, k, v, g)
            y_s, st_s = _ret_sample(ret_consts, q, k, v, g, state_ret[o])
            y = jnp.concatenate([y_p, y_m, y_s], 0)
            h, xn = _out_proj([y], [w_out_odd[o].astype(BF16)], h, norm_ffn[layer], 2, BF16)
            p_ret.append(st_p)
            s_ret.append(st_s)
        act = _ffn_up(xn, ffn_w_gate[layer].astype(BF16), ffn_w_up[layer].astype(BF16))
        last = layer == DEPTH - 1
        gain = norm_final if last else norm_mix[layer + 1]
        h, xn = _out_proj([act], [ffn_w_down[layer].astype(BF16)], h, gain, 4, F32 if last else BF16)
    y_prompt = xn[:N_PROMPT_ROWS].reshape(BATCH, SEQ, D_MODEL)
    y_sample = xn[ROW_SAMPLE:].reshape(DEC_BATCH, DEC_SEQ, D_MODEL)
    return (y_prompt, y_sample, jnp.stack(p_ckv), jnp.stack(p_kpe), jnp.stack(s_ckv), jnp.stack(s_kpe),
            jnp.stack(p_conv), jnp.stack(s_conv), jnp.stack(p_ret), jnp.stack(s_ret))
```

```python
import functools

import jax
import jax.numpy as jnp
from jax import lax
from jax.experimental import pallas as pl
from jax.experimental.pallas import tpu as pltpu

F32 = jnp.float32
BF16 = jnp.bfloat16

D_MODEL = 2048
BATCH = 4
SEQ = 2048
DEPTH = 4
DEC_BATCH = 128
DEC_SEQ = 8
PAST_LEN = 8192
PAGE_SIZE = 128
N_META = 16
N_EVEN = 2
N_ODD = 2
EPS = 1e-6
ROPE_BASE = 10000.0
CONV_CH = 1024
CONV_WIDTH = 31
MLA_HEADS = 8
Q_RANK = 512
KV_RANK = 512
NOPE_DIM = 128
ROPE_DIM = 64
V_DIM = 128
QK_DIM = NOPE_DIM + ROPE_DIM
MLA_SCALE = QK_DIM ** -0.5
RET_HEADS = 8
RET_DK = D_MODEL // RET_HEADS
RET_DV = 2 * RET_DK
RET_CHUNK = 128
RET_HK = RET_HEADS * RET_DK
RET_HV = RET_HEADS * RET_DV
D_FF = -(-8 * D_MODEL // (3 * 256)) * 256

N_PROMPT_ROWS = BATCH * SEQ
N_META_ROWS = BATCH * N_META
N_SAMPLE_ROWS = DEC_BATCH * DEC_SEQ
ROW_META = N_PROMPT_ROWS
ROW_SAMPLE = N_PROMPT_ROWS + N_META_ROWS
N_ROWS = ROW_SAMPLE + N_SAMPLE_ROWS
N_PAGES = PAST_LEN // PAGE_SIZE

LANES = 128
HEAD_PAD = 2 * LANES
VMEM_LIMIT = 56 * 1024 * 1024
TM_WIDE = 928
TM_NARROW = 464
TM_FFN = 1856
CONV_ROWS = 512
CONV_HALO = 32
CONV_CHUNK = 32
ATTN_BLOCK = 512
PAGES_PER_STEP = 32
PAGES_PER_GROUP = 16
RET_SEQS_PER_STEP = 8
CONV_SEQS_PER_STEP = 8
V_UP_SEQS_PER_STEP = 8


def _params(*semantics):
    return pltpu.CompilerParams(dimension_semantics=semantics, vmem_limit_bytes=VMEM_LIMIT)


def _dot(a, b):
    return jnp.dot(a, b, preferred_element_type=F32)


def _dot_nt(a, b):
    return lax.dot_general(a, b, (((1,), (1,)), ((), ())), preferred_element_type=F32)


def _dot_tn(a, b):
    return lax.dot_general(a, b, (((0,), (0,)), ((), ())), preferred_element_type=F32)


def _rms(x, gain):
    return x * lax.rsqrt(jnp.mean(x * x, -1, keepdims=True) + EPS) * gain


def _rmsnorm_kernel(x_ref, g_ref, o_ref):
    o_ref[...] = _rms(x_ref[...], g_ref[...]).astype(o_ref.dtype)


def _rmsnorm_rows(x, gain):
    tm = TM_WIDE
    return pl.pallas_call(
        _rmsnorm_kernel,
        out_shape=jax.ShapeDtypeStruct((N_ROWS, D_MODEL), BF16),
        grid=(N_ROWS // tm,),
        in_specs=[pl.BlockSpec((tm, D_MODEL), lambda i: (i, 0)),
                  pl.BlockSpec((1, D_MODEL), lambda i: (0, 0))],
        out_specs=pl.BlockSpec((tm, D_MODEL), lambda i: (i, 0)),
        compiler_params=_params("parallel"),
        name="rmsnorm_rows",
    )(x, gain.reshape(1, D_MODEL))


def _glu_kernel(x_ref, wa_ref, wg_ref, o_ref):
    x = x_ref[...]
    o_ref[...] = _dot(x, wa_ref[...]) * jax.nn.sigmoid(_dot(x, wg_ref[...]))


def _glu_proj(xn, w):
    tm, tn = TM_WIDE, 512
    nj = CONV_CH // tn
    return pl.pallas_call(
        _glu_kernel,
        out_shape=jax.ShapeDtypeStruct((N_ROWS, CONV_CH), F32),
        grid=(N_ROWS // tm, nj),
        in_specs=[pl.BlockSpec((tm, D_MODEL), lambda i, j: (i, 0)),
                  pl.BlockSpec((D_MODEL, tn), lambda i, j: (0, j)),
                  pl.BlockSpec((D_MODEL, tn), lambda i, j: (0, j + nj))],
        out_specs=pl.BlockSpec((tm, tn), lambda i, j: (i, j)),
        compiler_params=_params("parallel", "arbitrary"),
        name="glu_proj",
    )(xn, w, w)


def _latent_kernel(x_ref, wl_ref, qn_ref, kvn_ref, wq_ref, wqr_ref, wkv_ref, cos_ref, sin_ref,
                   q_ref, kv_ref, ckv_ref, kpe_ref, kpeb_ref):
    z = _dot(x_ref[...], wl_ref[...])
    cos = cos_ref[...]
    sin = sin_ref[...]
    cqn = _rms(z[:, :Q_RANK], qn_ref[...]).astype(BF16)
    ckvn = _rms(z[:, Q_RANK:Q_RANK + KV_RANK], kvn_ref[...])
    ckv_ref[...] = ckvn
    k0 = Q_RANK + KV_RANK
    kpe = z[:, k0:k0 + LANES] * cos + z[:, k0 + LANES:k0 + 2 * LANES] * sin
    kpe_ref[...] = kpe
    kpeb_ref[...] = kpe.astype(BF16)
    q = _dot(cqn, wq_ref[...])
    qr = _dot(cqn, wqr_ref[...])
    for h in range(MLA_HEADS):
        c0 = h * HEAD_PAD
        q_ref[:, c0:c0 + LANES] = q[:, c0:c0 + LANES].astype(BF16)
        pe = q[:, c0 + LANES:c0 + HEAD_PAD] * cos + qr[:, h * LANES:(h + 1) * LANES] * sin
        q_ref[:, c0 + LANES:c0 + HEAD_PAD] = pe.astype(BF16)
    kv_ref[...] = _dot(ckvn.astype(BF16), wkv_ref[...]).astype(BF16)


def _latent_proj(xn, wl, q_norm, kv_norm, wq, wqr, wkv, cos, sin):
    tm = TM_NARROW
    nl = wl.shape[1]
    full = lambda shape: pl.BlockSpec(shape, lambda i: (0, 0))
    rows = lambda n: pl.BlockSpec((tm, n), lambda i: (i, 0))
    return pl.pallas_call(
        _latent_kernel,
        out_shape=(jax.ShapeDtypeStruct((N_ROWS, MLA_HEADS * HEAD_PAD), BF16),
                   jax.ShapeDtypeStruct((N_ROWS, 2 * MLA_HEADS * LANES), BF16),
                   jax.ShapeDtypeStruct((N_ROWS, KV_RANK), F32),
                   jax.ShapeDtypeStruct((N_ROWS, LANES), F32),
                   jax.ShapeDtypeStruct((N_ROWS, LANES), BF16)),
        grid=(N_ROWS // tm,),
        in_specs=[rows(D_MODEL), full((D_MODEL, nl)), full((1, Q_RANK)), full((1, KV_RANK)),
                  full(wq.shape), full(wqr.shape), full(wkv.shape), rows(LANES), rows(LANES)],
        out_specs=(rows(MLA_HEADS * HEAD_PAD), rows(2 * MLA_HEADS * LANES), rows(KV_RANK),
                   rows(LANES), rows(LANES)),
        compiler_params=_params("parallel"),
        name="latent_proj",
    )(xn, wl, q_norm.reshape(1, Q_RANK), kv_norm.reshape(1, KV_RANK), wq, wqr, wkv, cos, sin)


def _rope_halves(t, cos, sin):
    half = RET_DK // 2
    outs = []
    for h in range(t.shape[1] // RET_DK):
        x1 = t[:, h * RET_DK:h * RET_DK + half]
        x2 = t[:, h * RET_DK + half:(h + 1) * RET_DK]
        outs += [x1 * cos - x2 * sin, x2 * cos + x1 * sin]
    return jnp.concatenate(outs, -1)


def _qk_kernel(x_ref, wq_ref, wk_ref, cos_ref, sin_ref, q_ref, k_ref):
    x = x_ref[...]
    cos = cos_ref[...]
    sin = sin_ref[...]
    q_ref[...] = _rope_halves(_dot(x, wq_ref[...].astype(BF16)), cos, sin).astype(BF16)
    k_ref[...] = _rope_halves(_dot(x, wk_ref[...].astype(BF16)), cos, sin) * (RET_DK ** -0.5)


def _ret_qk_proj(xn, w_all, o, cos, sin):
    tm, tn = TM_WIDE, 512
    nj = RET_HK // tn
    return pl.pallas_call(
        _qk_kernel,
        out_shape=(jax.ShapeDtypeStruct((N_ROWS, RET_HK), BF16),
                   jax.ShapeDtypeStruct((N_ROWS, RET_HK), F32)),
        grid=(N_ROWS // tm, nj),
        in_specs=[pl.BlockSpec((tm, D_MODEL), lambda i, j: (i, 0)),
                  pl.BlockSpec((None, D_MODEL, tn), lambda i, j: (o, 0, j)),
                  pl.BlockSpec((None, D_MODEL, tn), lambda i, j: (o, 0, j + nj)),
                  pl.BlockSpec((tm, RET_DK // 2), lambda i, j: (i, 0)),
                  pl.BlockSpec((tm, RET_DK // 2), lambda i, j: (i, 0))],
        out_specs=(pl.BlockSpec((tm, tn), lambda i, j: (i, j)),
                   pl.BlockSpec((tm, tn), lambda i, j: (i, j))),
        compiler_params=_params("parallel", "arbitrary"),
        name="ret_qk_proj",
    )(xn, w_all, w_all, cos, sin)


def _vg_kernel(x_ref, wv_ref, wg_ref, v_ref, g_ref):
    x = x_ref[...]
    v_ref[...] = _dot(x, wv_ref[...].astype(BF16)).astype(BF16)
    g_ref[...] = _dot(x, wg_ref[...].astype(BF16))


def _ret_vg_proj(xn, w_all, o):
    tm, tn = TM_WIDE, 512
    nj = RET_HV // tn
    off = 2 * RET_HK // tn
    return pl.pallas_call(
        _vg_kernel,
        out_shape=(jax.ShapeDtypeStruct((N_ROWS, RET_HV), BF16),
                   jax.ShapeDtypeStruct((N_ROWS, RET_HV), F32)),
        grid=(N_ROWS // tm, nj),
        in_specs=[pl.BlockSpec((tm, D_MODEL), lambda i, j: (i, 0)),
                  pl.BlockSpec((None, D_MODEL, tn), lambda i, j: (o, 0, off + j)),
                  pl.BlockSpec((None, D_MODEL, tn), lambda i, j: (o, 0, off + nj + j))],
        out_specs=(pl.BlockSpec((tm, tn), lambda i, j: (i, j)),
                   pl.BlockSpec((tm, tn), lambda i, j: (i, j))),
        compiler_params=_params("parallel", "arbitrary"),
        name="ret_vg_proj",
    )(xn, w_all, w_all)


def _ffn_up_kernel(x_ref, wg_ref, wu_ref, o_ref):
    x = x_ref[...]
    gate = _dot(x, wg_ref[...].astype(BF16))
    o_ref[...] = (jax.nn.silu(gate) * _dot(x, wu_ref[...].astype(BF16))).astype(BF16)


def _ffn_up(xn, wg_all, wu_all, layer):
    tm, tn = TM_FFN, 512
    return pl.pallas_call(
        _ffn_up_kernel,
        out_shape=jax.ShapeDtypeStruct((N_ROWS, D_FF), BF16),
        grid=(N_ROWS // tm, D_FF // tn),
        in_specs=[pl.BlockSpec((tm, D_MODEL), lambda i, j: (i, 0)),
                  pl.BlockSpec((None, D_MODEL, tn), lambda i, j: (layer, 0, j)),
                  pl.BlockSpec((None, D_MODEL, tn), lambda i, j: (layer, 0, j))],
        out_specs=pl.BlockSpec((tm, tn), lambda i, j: (i, j)),
        compiler_params=_params("parallel", "arbitrary"),
        name="ffn_up",
    )(xn, wg_all, wu_all)


def _out_proj_kernel(n_lhs, nk, *refs):
    a_refs = refs[:n_lhs]
    w_refs = refs[n_lhs:2 * n_lhs]
    h_ref, g_ref, ho_ref, no_ref = refs[2 * n_lhs:]
    k = pl.program_id(1)
    part = _dot(a_refs[0][...], w_refs[0][...])
    for a_ref, w_ref in zip(a_refs[1:], w_refs[1:]):
        part = part + _dot(a_ref[...], w_ref[...])

    @pl.when(k == 0)
    def _():
        ho_ref[...] = h_ref[...] + part

    @pl.when(k > 0)
    def _():
        ho_ref[...] += part

    @pl.when(k == nk - 1)
    def _():
        no_ref[...] = _rms(ho_ref[...], g_ref[...]).astype(no_ref.dtype)


def _out_proj(lhs, w_all, layer, h, gain, nk, norm_dtype):
    tm = TM_NARROW
    n = len(lhs)
    in_specs = []
    for a in lhs:
        in_specs.append(pl.BlockSpec((tm, a.shape[1] // nk), lambda i, k: (i, k)))
    ws = []
    row0 = 0
    for a in lhs:
        tk = a.shape[1] // nk
        blk0 = row0 // tk
        in_specs.append(pl.BlockSpec((None, tk, D_MODEL), lambda i, k, blk0=blk0: (layer, blk0 + k, 0)))
        ws.append(w_all)
        row0 += a.shape[1]
    in_specs += [pl.BlockSpec((tm, D_MODEL), lambda i, k: (i, 0)),
                 pl.BlockSpec((1, D_MODEL), lambda i, k: (0, 0))]
    return pl.pallas_call(
        functools.partial(_out_proj_kernel, n, nk),
        out_shape=(jax.ShapeDtypeStruct((N_ROWS, D_MODEL), F32),
                   jax.ShapeDtypeStruct((N_ROWS, D_MODEL), norm_dtype)),
        grid=(N_ROWS // tm, nk),
        in_specs=in_specs,
        out_specs=(pl.BlockSpec((tm, D_MODEL), lambda i, k: (i, 0)),
                   pl.BlockSpec((tm, D_MODEL), lambda i, k: (i, 0))),
        compiler_params=_params("parallel", "arbitrary"),
        name="out_proj",
    )(*lhs, *ws, h, gain.reshape(1, D_MODEL))


def _conv_taps(win, w_ref, lane0, rows):
    shift0 = CONV_HALO - (CONV_WIDTH - 1)
    acc = None
    for s in range(8):
        n_a = (CONV_WIDTH - 1 - s) // 8 + 1
        shifted = win[shift0 + s:shift0 + s + rows + 8 * (n_a - 1), :]
        for a in range(n_a):
            w = 8 * a + s
            term = shifted[8 * a:8 * a + rows, :] * w_ref[w:w + 1, lane0:lane0 + LANES]
            acc = term if acc is None else acc + term
    return acc


def _conv_post(y, b_ref, g_ref, beta_ref):
    y = y + b_ref[...]
    mu = jnp.mean(y, -1, keepdims=True)
    yc = y - mu
    var = jnp.mean(yc * yc, -1, keepdims=True)
    return jax.nn.silu(yc * lax.rsqrt(var + EPS) * g_ref[...] + beta_ref[...])


def _conv_prompt_kernel(cur_ref, prev_ref, meta_ref, w_ref, b_ref, g_ref, beta_ref,
                        o_ref, ometa_ref, full_ref):
    t = pl.program_id(1)
    full_ref[CONV_HALO:, :] = cur_ref[...]

    @pl.when(t == 0)
    def _():
        full_ref[0:CONV_HALO - N_META, :] = jnp.zeros((CONV_HALO - N_META, CONV_CH), F32)
        full_ref[CONV_HALO - N_META:CONV_HALO, :] = meta_ref[...]
        mwin = jnp.concatenate([jnp.zeros((CONV_HALO, CONV_CH), F32), meta_ref[...]], 0)
        parts = [_conv_taps(mwin[:, l * LANES:(l + 1) * LANES], w_ref, l * LANES, N_META)
                 for l in range(CONV_CH // LANES)]
        ometa_ref[...] = _conv_post(jnp.concatenate(parts, -1), b_ref, g_ref, beta_ref).astype(BF16)

    @pl.when(t > 0)
    def _():
        full_ref[0:CONV_HALO, :] = prev_ref[...]

    def chunk(c, carry):
        base = pl.multiple_of(c * CONV_CHUNK, CONV_CHUNK)
        parts = []
        for l in range(CONV_CH // LANES):
            win = full_ref[pl.ds(base, CONV_CHUNK + CONV_HALO), l * LANES:(l + 1) * LANES]
            parts.append(_conv_taps(win, w_ref, l * LANES, CONV_CHUNK))
        y = _conv_post(jnp.concatenate(parts, -1), b_ref, g_ref, beta_ref)
        o_ref[pl.ds(base, CONV_CHUNK), :] = y.astype(BF16)
        return carry

    lax.fori_loop(0, CONV_ROWS // CONV_CHUNK, chunk, 0)


def _put_meta_kernel(rows_ref, full_ref, o_ref):
    del full_ref
    o_ref[...] = rows_ref[...]


def _put_meta_rows(full, meta_rows):
    width = full.shape[1]
    return pl.pallas_call(
        _put_meta_kernel,
        out_shape=jax.ShapeDtypeStruct(full.shape, full.dtype),
        grid=(1,),
        in_specs=[pl.BlockSpec((N_META_ROWS, width), lambda i: (0, 0)),
                  pl.BlockSpec(memory_space=pl.ANY)],
        out_specs=pl.BlockSpec((N_META_ROWS, width), lambda i: (ROW_META // N_META_ROWS, 0)),
        input_output_aliases={1: 0},
        compiler_params=_params("arbitrary"),
        name="put_meta_rows",
    )(meta_rows, full)


def _conv_prompt(u, conv_w, conv_b, ln_g, ln_b):
    nt = SEQ // CONV_ROWS
    vec = lambda: pl.BlockSpec((1, CONV_CH), lambda b, t: (0, 0))
    return pl.pallas_call(
        _conv_prompt_kernel,
        out_shape=(jax.ShapeDtypeStruct((N_ROWS, CONV_CH), BF16),
                   jax.ShapeDtypeStruct((N_META_ROWS, CONV_CH), BF16)),
        grid=(BATCH, nt),
        in_specs=[pl.BlockSpec((CONV_ROWS, CONV_CH), lambda b, t: (b * nt + t, 0)),
                  pl.BlockSpec((CONV_HALO, CONV_CH),
                               lambda b, t: (jnp.maximum((b * SEQ + t * CONV_ROWS) // CONV_HALO - 1, 0), 0)),
                  pl.BlockSpec((N_META, CONV_CH), lambda b, t: (ROW_META // N_META + b, 0)),
                  pl.BlockSpec((CONV_WIDTH, CONV_CH), lambda b, t: (0, 0)),
                  vec(), vec(), vec()],
        out_specs=(pl.BlockSpec((CONV_ROWS, CONV_CH), lambda b, t: (b * nt + t, 0)),
                   pl.BlockSpec((N_META, CONV_CH), lambda b, t: (b, 0))),
        scratch_shapes=[pltpu.VMEM((CONV_ROWS + CONV_HALO, CONV_CH), F32)],
        compiler_params=_params("parallel", "arbitrary"),
        name="conv_prompt",
    )(u, u, u, conv_w, conv_b.reshape(1, CONV_CH), ln_g.reshape(1, CONV_CH), ln_b.reshape(1, CONV_CH))


def _conv_sample_kernel(u_ref, st_ref, w_ref, b_ref, g_ref, beta_ref, co_ref, o_ref, ns_ref, full_ref):
    del co_ref
    keep = CONV_WIDTH - 1
    shift0 = CONV_HALO - keep
    ys = []
    for s in range(CONV_SEQS_PER_STEP):
        u = u_ref[s * DEC_SEQ:(s + 1) * DEC_SEQ, :]
        full_ref[0:8, :] = jnp.zeros((8, CONV_CH), F32)
        full_ref[shift0:CONV_HALO, :] = st_ref[s]
        full_ref[CONV_HALO:, :] = u
        parts = [_conv_taps(full_ref[:, l * LANES:(l + 1) * LANES], w_ref, l * LANES, DEC_SEQ)
                 for l in range(CONV_CH // LANES)]
        ys.append(_conv_post(jnp.concatenate(parts, -1), b_ref, g_ref, beta_ref))
        ns_ref[s, 0:keep - DEC_SEQ, :] = st_ref[s, DEC_SEQ:keep, :]
        ns_ref[s, keep - DEC_SEQ:keep, :] = u
    o_ref[...] = jnp.concatenate(ys, 0).astype(BF16)


def _conv_sample(u, conv_full, state, conv_w, conv_b, ln_g, ln_b):
    g = CONV_SEQS_PER_STEP
    rows = g * DEC_SEQ
    r0 = ROW_SAMPLE // rows
    keep = CONV_WIDTH - 1
    vec = lambda: pl.BlockSpec((1, CONV_CH), lambda i: (0, 0))
    return pl.pallas_call(
        _conv_sample_kernel,
        out_shape=(jax.ShapeDtypeStruct((N_ROWS, CONV_CH), BF16),
                   jax.ShapeDtypeStruct((DEC_BATCH, keep, CONV_CH), F32)),
        grid=(DEC_BATCH // g,),
        in_specs=[pl.BlockSpec((rows, CONV_CH), lambda i: (r0 + i, 0)),
                  pl.BlockSpec((g, keep, CONV_CH), lambda i: (i, 0, 0)),
                  pl.BlockSpec((CONV_WIDTH, CONV_CH), lambda i: (0, 0)),
                  vec(), vec(), vec(),
                  pl.BlockSpec(memory_space=pl.ANY)],
        out_specs=(pl.BlockSpec((rows, CONV_CH), lambda i: (r0 + i, 0)),
                   pl.BlockSpec((g, keep, CONV_CH), lambda i: (i, 0, 0))),
        scratch_shapes=[pltpu.VMEM((CONV_HALO + DEC_SEQ, CONV_CH), F32)],
        input_output_aliases={6: 0},
        compiler_params=_params("parallel"),
        name="conv_sample",
    )(u, state, conv_w, conv_b.reshape(1, CONV_CH), ln_g.reshape(1, CONV_CH), ln_b.reshape(1, CONV_CH),
      conv_full)


def _softmax_step(q, kc, vc, m, l, acc, mask=None):
    s = _dot_nt(q, kc) * MLA_SCALE
    if mask is not None:
        s = jnp.where(mask, s, -jnp.inf)
    m_new = jnp.maximum(m, jnp.max(s, -1, keepdims=True))
    alpha = jnp.exp(m - m_new)
    p = jnp.exp(s - m_new)
    l = alpha * l + jnp.sum(p, -1, keepdims=True)
    acc = alpha * acc + _dot(p.astype(BF16), vc)
    return m_new, l, acc


def _causal_mask(n):
    return lax.broadcasted_iota(jnp.int32, (n, n), 0) >= lax.broadcasted_iota(jnp.int32, (n, n), 1)


def _attn_prompt_kernel(q_ref, qm_ref, kn_ref, kp_ref, v_ref, knm_ref, kpm_ref, vm_ref, o_ref, om_ref):
    qi = pl.program_id(2)
    blk = ATTN_BLOCK
    k_meta = jnp.concatenate([knm_ref[...], kpm_ref[...]], -1)
    v_meta = vm_ref[...]

    @pl.when(qi == 0)
    def _():
        init = (jnp.full((N_META, 1), -jnp.inf, F32), jnp.zeros((N_META, 1), F32),
                jnp.zeros((N_META, V_DIM), F32))
        _, l, acc = _softmax_step(qm_ref[...], k_meta, v_meta, *init, mask=_causal_mask(N_META))
        om_ref[...] = (acc / l).astype(BF16)

    q = q_ref[...]
    init = (jnp.full((blk, 1), -jnp.inf, F32), jnp.zeros((blk, 1), F32), jnp.zeros((blk, V_DIM), F32))
    carry = _softmax_step(q, k_meta, v_meta, *init)

    def keys(j):
        r0 = pl.multiple_of(j * blk, blk)
        kc = jnp.concatenate([kn_ref[pl.ds(r0, blk), :], kp_ref[pl.ds(r0, blk), :]], -1)
        return kc, v_ref[pl.ds(r0, blk), :]

    def body(j, c):
        kc, vc = keys(j)
        return _softmax_step(q, kc, vc, *c)

    carry = lax.fori_loop(0, qi, body, carry)
    kc, vc = keys(qi)
    _, l, acc = _softmax_step(q, kc, vc, *carry, mask=_causal_mask(blk))
    o_ref[...] = (acc / l).astype(BF16)


def _attn_prompt(q, kv, kpe):
    nq = SEQ // ATTN_BLOCK
    mrow = ROW_META // N_META
    return pl.pallas_call(
        _attn_prompt_kernel,
        out_shape=(jax.ShapeDtypeStruct((N_ROWS, MLA_HEADS * V_DIM), BF16),
                   jax.ShapeDtypeStruct((N_META_ROWS, MLA_HEADS * V_DIM), BF16)),
        grid=(BATCH, MLA_HEADS, nq),
        in_specs=[pl.BlockSpec((ATTN_BLOCK, HEAD_PAD), lambda b, h, i: (b * nq + i, h)),
                  pl.BlockSpec((N_META, HEAD_PAD), lambda b, h, i: (mrow + b, h)),
                  pl.BlockSpec((SEQ, LANES), lambda b, h, i: (b, h)),
                  pl.BlockSpec((SEQ, LANES), lambda b, h, i: (b, 0)),
                  pl.BlockSpec((SEQ, LANES), lambda b, h, i: (b, MLA_HEADS + h)),
                  pl.BlockSpec((N_META, LANES), lambda b, h, i: (mrow + b, h)),
                  pl.BlockSpec((N_META, LANES), lambda b, h, i: (mrow + b, 0)),
                  pl.BlockSpec((N_META, LANES), lambda b, h, i: (mrow + b, MLA_HEADS + h))],
        out_specs=(pl.BlockSpec((ATTN_BLOCK, V_DIM), lambda b, h, i: (b * nq + i, h)),
                   pl.BlockSpec((N_META, V_DIM), lambda b, h, i: (b, h))),
        compiler_params=_params("parallel", "parallel", "arbitrary"),
        name="attn_prompt",
    )(q, q, kv, kpe, kv, kv, kpe, kv)


def _absorb_q_kernel(q_ref, wuk_ref, qlat_ref, qpe_ref):
    q = q_ref[...]
    qlat = _dot(q[:, :NOPE_DIM], wuk_ref[...])
    qlat_ref[...] = qlat.reshape(DEC_BATCH, DEC_SEQ, KV_RANK)
    qpe_ref[...] = q[:, NOPE_DIM:].astype(F32).reshape(DEC_BATCH, DEC_SEQ, LANES)


def _absorb_q(q_s, wuk_t):
    rows = MLA_HEADS * DEC_SEQ
    return pl.pallas_call(
        _absorb_q_kernel,
        out_shape=(jax.ShapeDtypeStruct((DEC_BATCH, rows, KV_RANK), F32),
                   jax.ShapeDtypeStruct((DEC_BATCH, rows, LANES), F32)),
        grid=(MLA_HEADS,),
        in_specs=[pl.BlockSpec((N_SAMPLE_ROWS, HEAD_PAD), lambda h: (0, h)),
                  pl.BlockSpec((None, NOPE_DIM, KV_RANK), lambda h: (h, 0, 0))],
        out_specs=(pl.BlockSpec((DEC_BATCH, DEC_SEQ, KV_RANK), lambda h: (0, h, 0)),
                   pl.BlockSpec((DEC_BATCH, DEC_SEQ, LANES), lambda h: (0, h, 0))),
        compiler_params=_params("parallel"),
        name="absorb_q",
    )(q_s, wuk_t)


def _decode_kernel(pt_ref, qlat_ref, qpe_ref, *refs):
    g = PAGES_PER_STEP
    c_refs = refs[:g]
    k_refs = refs[g:2 * g]
    cn_ref, kn_ref, o_ref, m_ref, l_ref, acc_ref = refs[2 * g:]
    p_id = pl.program_id(1)
    rows = MLA_HEADS * DEC_SEQ

    @pl.when(p_id == 0)
    def _():
        m_ref[...] = jnp.full((rows, 1), -jnp.inf, F32)
        l_ref[...] = jnp.zeros((rows, 1), F32)
        acc_ref[...] = jnp.zeros((rows, KV_RANK), F32)

    ql = qlat_ref[...].astype(BF16)
    qp = qpe_ref[...][:, :ROPE_DIM].astype(BF16)
    m, l, acc = m_ref[...], l_ref[...], acc_ref[...]
    for g0 in range(0, g, PAGES_PER_GROUP):
        group = range(g0, g0 + PAGES_PER_GROUP)
        cs = [c_refs[i][...].astype(BF16) for i in group]
        s = jnp.concatenate(
            [_dot_nt(ql, c) + _dot(qp, k_refs[i][...].astype(BF16)) for c, i in zip(cs, group)],
            -1) * MLA_SCALE
        m_new = jnp.maximum(m, jnp.max(s, -1, keepdims=True))
        alpha = jnp.exp(m - m_new)
        p = jnp.exp(s - m_new)
        l = alpha * l + jnp.sum(p, -1, keepdims=True)
        pv = _dot(p[:, :PAGE_SIZE].astype(BF16), cs[0])
        for i in range(1, PAGES_PER_GROUP):
            pv = pv + _dot(p[:, i * PAGE_SIZE:(i + 1) * PAGE_SIZE].astype(BF16), cs[i])
        acc = alpha * acc + pv
        m = m_new
    m_ref[...] = m
    l_ref[...] = l
    acc_ref[...] = acc

    @pl.when(p_id == pl.num_programs(1) - 1)
    def _():
        cn = cn_ref[...].astype(BF16)
        kn = kn_ref[...][:, :ROPE_DIM].astype(BF16)
        sn = (_dot_nt(ql, cn) + _dot_nt(qp, kn)) * MLA_SCALE
        tok = lax.broadcasted_iota(jnp.int32, (rows, DEC_SEQ), 0) % DEC_SEQ
        col = lax.broadcasted_iota(jnp.int32, (rows, DEC_SEQ), 1)
        sn = jnp.where(col <= tok, sn, -jnp.inf)
        m_prev = m_ref[...]
        m_fin = jnp.maximum(m_prev, jnp.max(sn, -1, keepdims=True))
        a = jnp.exp(m_prev - m_fin)
        pn = jnp.exp(sn - m_fin)
        l_fin = a * l_ref[...] + jnp.sum(pn, -1, keepdims=True)
        acc = a * acc_ref[...] + _dot(pn.astype(BF16), cn)
        o_ref[...] = acc / l_fin


def _decode_attn(page_table, qlat, qpe, cache_ckv, cache_kpe_t, li, ckv_rows, kpe_rows):
    g = PAGES_PER_STEP
    rows = MLA_HEADS * DEC_SEQ
    srow = ROW_SAMPLE // DEC_SEQ

    def page_spec(shape, j):
        return pl.BlockSpec((None, None) + shape,
                            lambda b, p, pt: (li, pt[b * N_PAGES + p * g + j], 0, 0))

    in_specs = [pl.BlockSpec((None, rows, KV_RANK), lambda b, p, pt: (b, 0, 0)),
                pl.BlockSpec((None, rows, LANES), lambda b, p, pt: (b, 0, 0))]
    in_specs += [page_spec((PAGE_SIZE, KV_RANK), j) for j in range(g)]
    in_specs += [page_spec((ROPE_DIM, PAGE_SIZE), j) for j in range(g)]
    in_specs += [pl.BlockSpec((DEC_SEQ, KV_RANK), lambda b, p, pt: (srow + b, 0)),
                 pl.BlockSpec((DEC_SEQ, LANES), lambda b, p, pt: (srow + b, 0))]
    grid_spec = pltpu.PrefetchScalarGridSpec(
        num_scalar_prefetch=1,
        grid=(DEC_BATCH, N_PAGES // g),
        in_specs=in_specs,
        out_specs=pl.BlockSpec((None, rows, KV_RANK), lambda b, p, pt: (b, 0, 0)),
        scratch_shapes=[pltpu.VMEM((rows, 1), F32), pltpu.VMEM((rows, 1), F32),
                        pltpu.VMEM((rows, KV_RANK), F32)],
    )
    return pl.pallas_call(
        _decode_kernel,
        out_shape=jax.ShapeDtypeStruct((DEC_BATCH, rows, KV_RANK), F32),
        grid_spec=grid_spec,
        compiler_params=_params("parallel", "arbitrary"),
        name="decode_attn",
    )(page_table.reshape(-1), qlat, qpe, *([cache_ckv] * g), *([cache_kpe_t] * g), ckv_rows, kpe_rows)


def _v_up_kernel(o_ref, w_ref, full_ref, a_ref):
    del full_ref
    rows = V_UP_SEQS_PER_STEP * DEC_SEQ
    outs = []
    for h in range(MLA_HEADS):
        o = o_ref[:, h * DEC_SEQ:(h + 1) * DEC_SEQ, :].reshape(rows, KV_RANK).astype(BF16)
        outs.append(_dot(o, w_ref[h]))
    a_ref[...] = jnp.concatenate(outs, -1).astype(BF16)


def _v_up(o_lat, wuv, attn_full):
    g = V_UP_SEQS_PER_STEP
    rows = g * DEC_SEQ
    r0 = ROW_SAMPLE // rows
    return pl.pallas_call(
        _v_up_kernel,
        out_shape=jax.ShapeDtypeStruct((N_ROWS, MLA_HEADS * V_DIM), BF16),
        grid=(DEC_BATCH // g,),
        in_specs=[pl.BlockSpec((g, MLA_HEADS * DEC_SEQ, KV_RANK), lambda i: (i, 0, 0)),
                  pl.BlockSpec((MLA_HEADS, KV_RANK, V_DIM), lambda i: (0, 0, 0)),
                  pl.BlockSpec(memory_space=pl.ANY)],
        out_specs=pl.BlockSpec((rows, MLA_HEADS * V_DIM), lambda i: (r0 + i, 0)),
        input_output_aliases={2: 0},
        compiler_params=_params("parallel"),
        name="v_up",
    )(o_lat, wuv, attn_full)


def _ret_tables(n, lg):
    i = lax.broadcasted_iota(jnp.int32, (n, 1), 0).astype(F32)
    rel = (lax.broadcasted_iota(jnp.int32, (n, n), 0) - lax.broadcasted_iota(jnp.int32, (n, n), 1)).astype(F32)
    decay = jnp.where(rel >= 0, jnp.exp(lg * jnp.maximum(rel, 0.0)), 0.0)
    q_scale = jnp.exp((i + 1.0) * lg)
    k_scale = jnp.exp((n - 1.0 - i) * lg)
    return decay, q_scale, k_scale


def _ret_chunk(q, k, v, g, s_old, tables, s_decay):
    decay, q_scale, k_scale = tables
    inner = _dot_nt(q, k.astype(BF16)) * decay
    o = _dot(inner.astype(BF16), v) + _dot(q, s_old.astype(BF16)) * q_scale
    s_new = s_decay * s_old + _dot_tn((k * k_scale).astype(BF16), v)
    mu = jnp.mean(o, -1, keepdims=True)
    oc = o - mu
    var = jnp.mean(oc * oc, -1, keepdims=True)
    y = jax.nn.silu(g) * (oc * lax.rsqrt(var + EPS))
    return y, s_new


def _ret_prompt_kernel(c_ref, q_ref, k_ref, v_ref, g_ref, qm_ref, km_ref, vm_ref, gm_ref,
                       y_ref, ym_ref, so_ref, s_ref):
    h = pl.program_id(1)
    lg = c_ref[h, 0]
    ym, s0 = _ret_chunk(qm_ref[...], km_ref[...], vm_ref[...], gm_ref[...],
                        jnp.zeros((RET_DK, RET_DV), F32), _ret_tables(N_META, lg), c_ref[h, 2])
    ym_ref[...] = ym.astype(BF16)
    s_ref[...] = s0
    tables = _ret_tables(RET_CHUNK, lg)
    s_decay = c_ref[h, 1]

    def body(c, carry):
        r0 = pl.multiple_of(c * RET_CHUNK, RET_CHUNK)
        rows = pl.ds(r0, RET_CHUNK)
        y, s_new = _ret_chunk(q_ref[rows, :], k_ref[rows, :], v_ref[rows, :], g_ref[rows, :],
                              s_ref[...], tables, s_decay)
        y_ref[rows, :] = y.astype(BF16)
        s_ref[...] = s_new
        return carry

    lax.fori_loop(0, SEQ // RET_CHUNK, body, 0, unroll=2)
    so_ref[...] = s_ref[...]


def _ret_prompt(consts, q, k, v, g):
    mrow = ROW_META // N_META
    seq = lambda w: pl.BlockSpec((SEQ, w), lambda b, h: (b, h))
    meta = lambda w: pl.BlockSpec((N_META, w), lambda b, h: (mrow + b, h))
    return pl.pallas_call(
        _ret_prompt_kernel,
        out_shape=(jax.ShapeDtypeStruct((N_ROWS, RET_HV), BF16),
                   jax.ShapeDtypeStruct((N_META_ROWS, RET_HV), BF16),
                   jax.ShapeDtypeStruct((BATCH, RET_HEADS, RET_DK, RET_DV), F32)),
        grid=(BATCH, RET_HEADS),
        in_specs=[pl.BlockSpec(memory_space=pltpu.SMEM),
                  seq(RET_DK), seq(RET_DK), seq(RET_DV), seq(RET_DV),
                  meta(RET_DK), meta(RET_DK), meta(RET_DV), meta(RET_DV)],
        out_specs=(seq(RET_DV), pl.BlockSpec((N_META, RET_DV), lambda b, h: (b, h)),
                   pl.BlockSpec((None, None, RET_DK, RET_DV), lambda b, h: (b, h, 0, 0))),
        scratch_shapes=[pltpu.VMEM((RET_DK, RET_DV), F32)],
        compiler_params=_params("parallel", "parallel"),
        name="ret_prompt",
    )(consts, q, k, v, g, q, k, v, g)


def _ret_sample_kernel(c_ref, q_ref, k_ref, v_ref, g_ref, s_ref, *rest):
    y_ref, so_ref = rest[-2:]
    h = pl.program_id(1)
    tables = _ret_tables(DEC_SEQ, c_ref[h, 0])
    s_decay = c_ref[h, 3]
    q_all = q_ref[...].astype(F32)
    k_all = k_ref[...]
    v_all = v_ref[...].astype(F32)
    g_all = g_ref[...]
    ys = []
    for s in range(RET_SEQS_PER_STEP):
        rows = slice(s * DEC_SEQ, (s + 1) * DEC_SEQ)
        y, s_new = _ret_chunk(q_all[rows].astype(BF16), k_all[rows], v_all[rows].astype(BF16), g_all[rows],
                              s_ref[s], tables, s_decay)
        ys.append(y)
        so_ref[s] = s_new
    y_ref[...] = jnp.concatenate(ys, 0).astype(BF16)


def _ret_sample(consts, q, k, v, g, y_full, state_all, o, new_states):
    gs = RET_SEQS_PER_STEP
    rows = gs * DEC_SEQ
    r0 = ROW_SAMPLE // rows
    tok = lambda w: pl.BlockSpec((rows, w), lambda i, h: (r0 + i, h))
    st = pl.BlockSpec((None, gs, None, RET_DK, RET_DV), lambda i, h: (o, i, h, 0, 0))
    in_specs = [pl.BlockSpec(memory_space=pltpu.SMEM), tok(RET_DK), tok(RET_DK), tok(RET_DV), tok(RET_DV), st,
                pl.BlockSpec(memory_space=pl.ANY)]
    args = [consts, q, k, v, g, state_all, y_full]
    aliases = {6: 0}
    if new_states is not None:
        in_specs.append(pl.BlockSpec(memory_space=pl.ANY))
        aliases[len(args)] = 1
        args.append(new_states)
    return pl.pallas_call(
        _ret_sample_kernel,
        out_shape=(jax.ShapeDtypeStruct((N_ROWS, RET_HV), BF16),
                   jax.ShapeDtypeStruct(state_all.shape, F32)),
        grid=(DEC_BATCH // gs, RET_HEADS),
        in_specs=in_specs,
        out_specs=(tok(RET_DV), st),
        input_output_aliases=aliases,
        compiler_params=_params("parallel", "parallel"),
        name="ret_sample",
    )(*args)


def _rotate_half_cols(w):
    half = w.shape[-1] // 2
    return jnp.concatenate([-w[..., half:], w[..., :half]], -1)


def _pad_cols(w, n):
    return jnp.pad(w, [(0, 0)] * (w.ndim - 1) + [(0, n - w.shape[-1])])


def _even_weights(w_in, w_uq, w_kv_up):
    conv_cols = 2 * CONV_CH
    w_glu = w_in[:, :conv_cols].astype(BF16)
    w_kpe = w_in[:, conv_cols + Q_RANK + KV_RANK:]
    w_lat = jnp.concatenate([w_in[:, conv_cols:conv_cols + Q_RANK + KV_RANK],
                             _pad_cols(w_kpe, LANES), _pad_cols(_rotate_half_cols(w_kpe), LANES)], -1).astype(BF16)
    w_q = _pad_cols(w_uq, HEAD_PAD).reshape(Q_RANK, MLA_HEADS * HEAD_PAD).astype(BF16)
    w_qr = _pad_cols(_rotate_half_cols(w_uq[..., NOPE_DIM:]), LANES).reshape(Q_RANK, MLA_HEADS * LANES).astype(BF16)
    w_uk = w_kv_up[..., :NOPE_DIM]
    w_uv = w_kv_up[..., NOPE_DIM:]
    w_kv = jnp.concatenate([w_uk.reshape(KV_RANK, -1), w_uv.reshape(KV_RANK, -1)], -1).astype(BF16)
    w_uk_t = jnp.transpose(w_uk, (1, 2, 0)).astype(BF16)
    w_uv_h = jnp.transpose(w_uv, (1, 0, 2)).astype(BF16)
    return w_glu, w_lat, w_q, w_qr, w_kv, w_uk_t, w_uv_h


def _rope_tables(pos, d):
    inv = ROPE_BASE ** (-jnp.arange(d // 2, dtype=F32) * 2.0 / d)
    ang = pos.astype(F32)[:, None] * inv
    return jnp.cos(ang), jnp.sin(ang)


def _ret_consts():
    lg = jnp.log1p(-(2.0 ** (-5.0 - jnp.arange(RET_HEADS, dtype=F32))))
    return jnp.stack([lg, jnp.exp(RET_CHUNK * lg), jnp.exp(N_META * lg), jnp.exp(DEC_SEQ * lg)], -1)


def kernel(x_prompt, x_sample, cache_ckv, cache_kpe, state_conv, state_ret, page_table, meta_tokens,
           norm_mix, norm_ffn, norm_final, w_in_even, conv_w, conv_b, conv_ln_g, conv_ln_b,
           mla_q_norm, mla_w_uq, mla_kv_norm, mla_w_kv_up, w_out_even, w_in_odd, w_out_odd,
           ffn_w_gate, ffn_w_up, ffn_w_down):
    h = jnp.concatenate([x_prompt.reshape(N_PROMPT_ROWS, D_MODEL),
                         jnp.tile(meta_tokens.astype(x_prompt.dtype), (BATCH, 1)),
                         x_sample.reshape(N_SAMPLE_ROWS, D_MODEL)], 0)
    pos = jnp.concatenate([jnp.tile(N_META + jnp.arange(SEQ), BATCH),
                           jnp.tile(jnp.arange(N_META), BATCH),
                           jnp.tile(PAST_LEN + jnp.arange(DEC_SEQ), DEC_BATCH)])
    cos_m, sin_m = _rope_tables(pos, ROPE_DIM)
    cos_mla = _pad_cols(jnp.concatenate([cos_m, cos_m], -1), LANES)
    sin_mla = _pad_cols(jnp.concatenate([sin_m, sin_m], -1), LANES)
    cos_ret, sin_ret = _rope_tables(pos, RET_DK)
    ret_consts = _ret_consts()

    cache_kpe_t = jnp.swapaxes(cache_kpe, 2, 3)
    w_out_even_b = w_out_even.astype(BF16)
    w_out_odd_b = w_out_odd.astype(BF16)
    ffn_w_down_b = ffn_w_down.astype(BF16)

    p_ckv, p_kpe, s_ckv, s_kpe, p_conv, s_conv, p_ret = [], [], [], [], [], [], []
    s_ret = None
    xn = _rmsnorm_rows(h, norm_mix[0])
    for layer in range(DEPTH):
        if layer % 2 == 0:
            e = layer // 2
            w_glu, w_lat, w_q, w_qr, w_kv, w_uk_t, w_uv_h = _even_weights(w_in_even[e], mla_w_uq[e], mla_w_kv_up[e])
            u = _glu_proj(xn, w_glu)
            q, kv, ckv, kpe, kpe_b = _latent_proj(xn, w_lat, mla_q_norm[e], mla_kv_norm[e],
                                                  w_q, w_qr, w_kv, cos_mla, sin_mla)
            conv_out, co_m = _conv_prompt(u, conv_w[e], conv_b[e], conv_ln_g[e], conv_ln_b[e])
            conv_out = _put_meta_rows(conv_out, co_m)
            conv_out, new_state = _conv_sample(u, conv_out, state_conv[e],
                                               conv_w[e], conv_b[e], conv_ln_g[e], conv_ln_b[e])
            attn, at_m = _attn_prompt(q, kv, kpe_b)
            attn = _put_meta_rows(attn, at_m)
            qlat, qpe = _absorb_q(q[ROW_SAMPLE:], w_uk_t)
            o_lat = _decode_attn(page_table, qlat, qpe, cache_ckv, cache_kpe_t, e, ckv, kpe)
            attn = _v_up(o_lat, w_uv_h, attn)
            h, xn = _out_proj([conv_out, attn], w_out_even_b, e, h, norm_ffn[layer], 1, BF16)
            u_p = u[:N_PROMPT_ROWS].reshape(BATCH, SEQ, CONV_CH)
            p_conv.append(u_p[:, SEQ - (CONV_WIDTH - 1):])
            s_conv.append(new_state)
            ckv_p = ckv[:N_PROMPT_ROWS].reshape(BATCH, SEQ, KV_RANK)
            ckv_m = ckv[ROW_META:ROW_SAMPLE].reshape(BATCH, N_META, KV_RANK)
            p_ckv.append(jnp.concatenate([ckv_m, ckv_p], 1))
            kpe_p = kpe[:N_PROMPT_ROWS, :ROPE_DIM].reshape(BATCH, SEQ, ROPE_DIM)
            kpe_m = kpe[ROW_META:ROW_SAMPLE, :ROPE_DIM].reshape(BATCH, N_META, ROPE_DIM)
            p_kpe.append(jnp.concatenate([kpe_m, kpe_p], 1))
            s_ckv.append(ckv[ROW_SAMPLE:].reshape(DEC_BATCH, DEC_SEQ, KV_RANK))
            s_kpe.append(kpe[ROW_SAMPLE:, :ROPE_DIM].reshape(DEC_BATCH, DEC_SEQ, ROPE_DIM))
        else:
            o = layer // 2
            q, k = _ret_qk_proj(xn, w_in_odd, o, cos_ret, sin_ret)
            v, g = _ret_vg_proj(xn, w_in_odd, o)
            y, y_m, st_p = _ret_prompt(ret_consts, q, k, v, g)
            y = _put_meta_rows(y, y_m)
            y, s_ret = _ret_sample(ret_consts, q, k, v, g, y, state_ret, o, s_ret)
            h, xn = _out_proj([y], w_out_odd_b, o, h, norm_ffn[layer], 2, BF16)
            p_ret.append(st_p)
        act = _ffn_up(xn, ffn_w_gate, ffn_w_up, layer)
        last = layer == DEPTH - 1
        gain = norm_final if last else norm_mix[layer + 1]
        h, xn = _out_proj([act], ffn_w_down_b, layer, h, gain, 4, F32 if last else BF16)
    y_prompt = xn[:N_PROMPT_ROWS].reshape(BATCH, SEQ, D_MODEL)
    y_sample = xn[ROW_SAMPLE:].reshape(DEC_BATCH, DEC_SEQ, D_MODEL)
    return (y_prompt, y_sample, jnp.stack(p_ckv), jnp.stack(p_kpe), jnp.stack(s_ckv), jnp.stack(s_kpe),
            jnp.stack(p_conv), jnp.stack(s_conv), jnp.stack(p_ret), s_ret)
```

```python
import functools

import jax
import jax.numpy as jnp
from jax import lax
from jax.experimental import pallas as pl
from jax.experimental.pallas import tpu as pltpu

F32 = jnp.float32
BF16 = jnp.bfloat16

D_MODEL = 2048
BATCH = 4
SEQ = 2048
DEPTH = 4
DEC_BATCH = 128
DEC_SEQ = 8
PAST_LEN = 8192
PAGE_SIZE = 128
N_META = 16
N_EVEN = 2
N_ODD = 2
EPS = 1e-6
ROPE_BASE = 10000.0
CONV_CH = 1024
CONV_WIDTH = 31
MLA_HEADS = 8
Q_RANK = 512
KV_RANK = 512
NOPE_DIM = 128
ROPE_DIM = 64
V_DIM = 128
QK_DIM = NOPE_DIM + ROPE_DIM
MLA_SCALE = QK_DIM ** -0.5
RET_HEADS = 8
RET_DK = D_MODEL // RET_HEADS
RET_DV = 2 * RET_DK
RET_CHUNK = 128
RET_HK = RET_HEADS * RET_DK
RET_HV = RET_HEADS * RET_DV
D_FF = -(-8 * D_MODEL // (3 * 256)) * 256

N_PROMPT_ROWS = BATCH * SEQ
N_META_ROWS = BATCH * N_META
N_SAMPLE_ROWS = DEC_BATCH * DEC_SEQ
ROW_META = N_PROMPT_ROWS
ROW_SAMPLE = N_PROMPT_ROWS + N_META_ROWS
N_ROWS = ROW_SAMPLE + N_SAMPLE_ROWS
N_PAGES = PAST_LEN // PAGE_SIZE

LANES = 128
HEAD_PAD = 2 * LANES
VMEM_LIMIT = 56 * 1024 * 1024
TM_WIDE = 928
TM_NARROW = 464
TM_FFN = 1856
CONV_ROWS = 512
CONV_HALO = 32
CONV_CHUNK = 32
ATTN_BLOCK = 512
PAGES_PER_STEP = 32
PAGES_PER_GROUP = 16
RET_SEQS_PER_STEP = 8
CONV_SEQS_PER_STEP = 8
V_UP_SEQS_PER_STEP = 8


def _params(*semantics):
    return pltpu.CompilerParams(dimension_semantics=semantics, vmem_limit_bytes=VMEM_LIMIT)


def _dot(a, b):
    return jnp.dot(a, b, preferred_element_type=F32)


def _dot_nt(a, b):
    return lax.dot_general(a, b, (((1,), (1,)), ((), ())), preferred_element_type=F32)


def _dot_tn(a, b):
    return lax.dot_general(a, b, (((0,), (0,)), ((), ())), preferred_element_type=F32)


def _rms(x, gain):
    return x * lax.rsqrt(jnp.mean(x * x, -1, keepdims=True) + EPS) * gain


def _rmsnorm_kernel(x_ref, g_ref, o_ref):
    o_ref[...] = _rms(x_ref[...], g_ref[...]).astype(o_ref.dtype)


def _rmsnorm_rows(x, gain):
    tm = TM_WIDE
    return pl.pallas_call(
        _rmsnorm_kernel,
        out_shape=jax.ShapeDtypeStruct((N_ROWS, D_MODEL), BF16),
        grid=(N_ROWS // tm,),
        in_specs=[pl.BlockSpec((tm, D_MODEL), lambda i: (i, 0)),
                  pl.BlockSpec((1, D_MODEL), lambda i: (0, 0))],
        out_specs=pl.BlockSpec((tm, D_MODEL), lambda i: (i, 0)),
        compiler_params=_params("parallel"),
        name="rmsnorm_rows",
    )(x, gain.reshape(1, D_MODEL))


def _glu_kernel(x_ref, wa_ref, wg_ref, o_ref):
    x = x_ref[...]
    o_ref[...] = _dot(x, wa_ref[...]) * jax.nn.sigmoid(_dot(x, wg_ref[...]))


def _glu_proj(xn, w):
    tm, tn = TM_WIDE, 512
    nj = CONV_CH // tn
    return pl.pallas_call(
        _glu_kernel,
        out_shape=jax.ShapeDtypeStruct((N_ROWS, CONV_CH), F32),
        grid=(N_ROWS // tm, nj),
        in_specs=[pl.BlockSpec((tm, D_MODEL), lambda i, j: (i, 0)),
                  pl.BlockSpec((D_MODEL, tn), lambda i, j: (0, j)),
                  pl.BlockSpec((D_MODEL, tn), lambda i, j: (0, j + nj))],
        out_specs=pl.BlockSpec((tm, tn), lambda i, j: (i, j)),
        compiler_params=_params("parallel", "arbitrary"),
        name="glu_proj",
    )(xn, w, w)


def _latent_kernel(x_ref, wl_ref, qn_ref, kvn_ref, wq_ref, wqr_ref, wkv_ref, cos_ref, sin_ref,
                   q_ref, kv_ref, ckv_ref, kpe_ref, kpeb_ref):
    z = _dot(x_ref[...], wl_ref[...])
    cos = cos_ref[...]
    sin = sin_ref[...]
    cqn = _rms(z[:, :Q_RANK], qn_ref[...]).astype(BF16)
    ckvn = _rms(z[:, Q_RANK:Q_RANK + KV_RANK], kvn_ref[...])
    ckv_ref[...] = ckvn
    k0 = Q_RANK + KV_RANK
    kpe = z[:, k0:k0 + LANES] * cos + z[:, k0 + LANES:k0 + 2 * LANES] * sin
    kpe_ref[...] = kpe
    kpeb_ref[...] = kpe.astype(BF16)
    q = _dot(cqn, wq_ref[...])
    qr = _dot(cqn, wqr_ref[...])
    for h in range(MLA_HEADS):
        c0 = h * HEAD_PAD
        q_ref[:, c0:c0 + LANES] = q[:, c0:c0 + LANES].astype(BF16)
        pe = q[:, c0 + LANES:c0 + HEAD_PAD] * cos + qr[:, h * LANES:(h + 1) * LANES] * sin
        q_ref[:, c0 + LANES:c0 + HEAD_PAD] = pe.astype(BF16)
    kv_ref[...] = _dot(ckvn.astype(BF16), wkv_ref[...]).astype(BF16)


def _latent_proj(xn, wl, q_norm, kv_norm, wq, wqr, wkv, cos, sin):
    tm = TM_NARROW
    nl = wl.shape[1]
    full = lambda shape: pl.BlockSpec(shape, lambda i: (0, 0))
    rows = lambda n: pl.BlockSpec((tm, n), lambda i: (i, 0))
    return pl.pallas_call(
        _latent_kernel,
        out_shape=(jax.ShapeDtypeStruct((N_ROWS, MLA_HEADS * HEAD_PAD), BF16),
                   jax.ShapeDtypeStruct((N_ROWS, 2 * MLA_HEADS * LANES), BF16),
                   jax.ShapeDtypeStruct((N_ROWS, KV_RANK), F32),
                   jax.ShapeDtypeStruct((N_ROWS, LANES), F32),
                   jax.ShapeDtypeStruct((N_ROWS, LANES), BF16)),
        grid=(N_ROWS // tm,),
        in_specs=[rows(D_MODEL), full((D_MODEL, nl)), full((1, Q_RANK)), full((1, KV_RANK)),
                  full(wq.shape), full(wqr.shape), full(wkv.shape), rows(LANES), rows(LANES)],
        out_specs=(rows(MLA_HEADS * HEAD_PAD), rows(2 * MLA_HEADS * LANES), rows(KV_RANK),
                   rows(LANES), rows(LANES)),
        compiler_params=_params("parallel"),
        name="latent_proj",
    )(xn, wl, q_norm.reshape(1, Q_RANK), kv_norm.reshape(1, KV_RANK), wq, wqr, wkv, cos, sin)


def _rope_halves(t, cos, sin):
    half = RET_DK // 2
    outs = []
    for h in range(t.shape[1] // RET_DK):
        x1 = t[:, h * RET_DK:h * RET_DK + half]
        x2 = t[:, h * RET_DK + half:(h + 1) * RET_DK]
        outs += [x1 * cos - x2 * sin, x2 * cos + x1 * sin]
    return jnp.concatenate(outs, -1)


def _qk_kernel(x_ref, wq_ref, wk_ref, cos_ref, sin_ref, q_ref, k_ref):
    x = x_ref[...]
    cos = cos_ref[...]
    sin = sin_ref[...]
    q_ref[...] = _rope_halves(_dot(x, wq_ref[...].astype(BF16)), cos, sin).astype(BF16)
    k_ref[...] = _rope_halves(_dot(x, wk_ref[...].astype(BF16)), cos, sin) * (RET_DK ** -0.5)


def _ret_qk_proj(xn, w_all, o, cos, sin):
    tm, tn = TM_WIDE, 512
    nj = RET_HK // tn
    return pl.pallas_call(
        _qk_kernel,
        out_shape=(jax.ShapeDtypeStruct((N_ROWS, RET_HK), BF16),
                   jax.ShapeDtypeStruct((N_ROWS, RET_HK), F32)),
        grid=(N_ROWS // tm, nj),
        in_specs=[pl.BlockSpec((tm, D_MODEL), lambda i, j: (i, 0)),
                  pl.BlockSpec((None, D_MODEL, tn), lambda i, j: (o, 0, j)),
                  pl.BlockSpec((None, D_MODEL, tn), lambda i, j: (o, 0, j + nj)),
                  pl.BlockSpec((tm, RET_DK // 2), lambda i, j: (i, 0)),
                  pl.BlockSpec((tm, RET_DK // 2), lambda i, j: (i, 0))],
        out_specs=(pl.BlockSpec((tm, tn), lambda i, j: (i, j)),
                   pl.BlockSpec((tm, tn), lambda i, j: (i, j))),
        compiler_params=_params("parallel", "arbitrary"),
        name="ret_qk_proj",
    )(xn, w_all, w_all, cos, sin)


def _vg_kernel(x_ref, wv_ref, wg_ref, v_ref, g_ref):
    x = x_ref[...]
    v_ref[...] = _dot(x, wv_ref[...].astype(BF16)).astype(BF16)
    g_ref[...] = _dot(x, wg_ref[...].astype(BF16))


def _ret_vg_proj(xn, w_all, o):
    tm, tn = TM_WIDE, 512
    nj = RET_HV // tn
    off = 2 * RET_HK // tn
    return pl.pallas_call(
        _vg_kernel,
        out_shape=(jax.ShapeDtypeStruct((N_ROWS, RET_HV), BF16),
                   jax.ShapeDtypeStruct((N_ROWS, RET_HV), F32)),
        grid=(N_ROWS // tm, nj),
        in_specs=[pl.BlockSpec((tm, D_MODEL), lambda i, j: (i, 0)),
                  pl.BlockSpec((None, D_MODEL, tn), lambda i, j: (o, 0, off + j)),
                  pl.BlockSpec((None, D_MODEL, tn), lambda i, j: (o, 0, off + nj + j))],
        out_specs=(pl.BlockSpec((tm, tn), lambda i, j: (i, j)),
                   pl.BlockSpec((tm, tn), lambda i, j: (i, j))),
        compiler_params=_params("parallel", "arbitrary"),
        name="ret_vg_proj",
    )(xn, w_all, w_all)


def _ffn_up_kernel(x_ref, wg_ref, wu_ref, o_ref):
    x = x_ref[...]
    gate = _dot(x, wg_ref[...].astype(BF16))
    o_ref[...] = (jax.nn.silu(gate) * _dot(x, wu_ref[...].astype(BF16))).astype(BF16)


def _ffn_up(xn, wg_all, wu_all, layer):
    tm, tn = TM_FFN, 512
    return pl.pallas_call(
        _ffn_up_kernel,
        out_shape=jax.ShapeDtypeStruct((N_ROWS, D_FF), BF16),
        grid=(N_ROWS // tm, D_FF // tn),
        in_specs=[pl.BlockSpec((tm, D_MODEL), lambda i, j: (i, 0)),
                  pl.BlockSpec((None, D_MODEL, tn), lambda i, j: (layer, 0, j)),
                  pl.BlockSpec((None, D_MODEL, tn), lambda i, j: (layer, 0, j))],
        out_specs=pl.BlockSpec((tm, tn), lambda i, j: (i, j)),
        compiler_params=_params("parallel", "arbitrary"),
        name="ffn_up",
    )(xn, wg_all, wu_all)


def _out_proj_kernel(n_lhs, nk, *refs):
    a_refs = refs[:n_lhs]
    w_refs = refs[n_lhs:2 * n_lhs]
    h_ref, g_ref, ho_ref, no_ref = refs[2 * n_lhs:]
    k = pl.program_id(1)
    part = _dot(a_refs[0][...], w_refs[0][...])
    for a_ref, w_ref in zip(a_refs[1:], w_refs[1:]):
        part = part + _dot(a_ref[...], w_ref[...])

    @pl.when(k == 0)
    def _():
        ho_ref[...] = h_ref[...] + part

    @pl.when(k > 0)
    def _():
        ho_ref[...] += part

    @pl.when(k == nk - 1)
    def _():
        no_ref[...] = _rms(ho_ref[...], g_ref[...]).astype(no_ref.dtype)


def _out_proj(lhs, w_all, layer, h, gain, nk, norm_dtype):
    tm = TM_NARROW
    n = len(lhs)
    in_specs = []
    for a in lhs:
        in_specs.append(pl.BlockSpec((tm, a.shape[1] // nk), lambda i, k: (i, k)))
    ws = []
    row0 = 0
    for a in lhs:
        tk = a.shape[1] // nk
        blk0 = row0 // tk
        in_specs.append(pl.BlockSpec((None, tk, D_MODEL), lambda i, k, blk0=blk0: (layer, blk0 + k, 0)))
        ws.append(w_all)
        row0 += a.shape[1]
    in_specs += [pl.BlockSpec((tm, D_MODEL), lambda i, k: (i, 0)),
                 pl.BlockSpec((1, D_MODEL), lambda i, k: (0, 0))]
    return pl.pallas_call(
        functools.partial(_out_proj_kernel, n, nk),
        out_shape=(jax.ShapeDtypeStruct((N_ROWS, D_MODEL), F32),
                   jax.ShapeDtypeStruct((N_ROWS, D_MODEL), norm_dtype)),
        grid=(N_ROWS // tm, nk),
        in_specs=in_specs,
        out_specs=(pl.BlockSpec((tm, D_MODEL), lambda i, k: (i, 0)),
                   pl.BlockSpec((tm, D_MODEL), lambda i, k: (i, 0))),
        compiler_params=_params("parallel", "arbitrary"),
        name="out_proj",
    )(*lhs, *ws, h, gain.reshape(1, D_MODEL))


def _conv_taps(win, w_ref, lane0, rows):
    shift0 = CONV_HALO - (CONV_WIDTH - 1)
    acc = None
    for s in range(8):
        n_a = (CONV_WIDTH - 1 - s) // 8 + 1
        shifted = win[shift0 + s:shift0 + s + rows + 8 * (n_a - 1), :]
        for a in range(n_a):
            w = 8 * a + s
            term = shifted[8 * a:8 * a + rows, :] * w_ref[w:w + 1, lane0:lane0 + LANES]
            acc = term if acc is None else acc + term
    return acc


def _conv_post(y, b_ref, g_ref, beta_ref):
    y = y + b_ref[...]
    mu = jnp.mean(y, -1, keepdims=True)
    yc = y - mu
    var = jnp.mean(yc * yc, -1, keepdims=True)
    return jax.nn.silu(yc * lax.rsqrt(var + EPS) * g_ref[...] + beta_ref[...])


def _conv_prompt_kernel(cur_ref, prev_ref, meta_ref, w_ref, b_ref, g_ref, beta_ref,
                        o_ref, ometa_ref, full_ref):
    t = pl.program_id(1)
    full_ref[CONV_HALO:, :] = cur_ref[...]

    @pl.when(t == 0)
    def _():
        full_ref[0:CONV_HALO - N_META, :] = jnp.zeros((CONV_HALO - N_META, CONV_CH), F32)
        full_ref[CONV_HALO - N_META:CONV_HALO, :] = meta_ref[...]
        mwin = jnp.concatenate([jnp.zeros((CONV_HALO, CONV_CH), F32), meta_ref[...]], 0)
        parts = [_conv_taps(mwin[:, l * LANES:(l + 1) * LANES], w_ref, l * LANES, N_META)
                 for l in range(CONV_CH // LANES)]
        ometa_ref[...] = _conv_post(jnp.concatenate(parts, -1), b_ref, g_ref, beta_ref).astype(BF16)

    @pl.when(t > 0)
    def _():
        full_ref[0:CONV_HALO, :] = prev_ref[...]

    def chunk(c, carry):
        base = pl.multiple_of(c * CONV_CHUNK, CONV_CHUNK)
        parts = []
        for l in range(CONV_CH // LANES):
            win = full_ref[pl.ds(base, CONV_CHUNK + CONV_HALO), l * LANES:(l + 1) * LANES]
            parts.append(_conv_taps(win, w_ref, l * LANES, CONV_CHUNK))
        y = _conv_post(jnp.concatenate(parts, -1), b_ref, g_ref, beta_ref)
        o_ref[pl.ds(base, CONV_CHUNK), :] = y.astype(BF16)
        return carry

    lax.fori_loop(0, CONV_ROWS // CONV_CHUNK, chunk, 0)


def _put_meta_kernel(rows_ref, full_ref, o_ref):
    del full_ref
    o_ref[...] = rows_ref[...]


def _put_meta_rows(full, meta_rows):
    width = full.shape[1]
    return pl.pallas_call(
        _put_meta_kernel,
        out_shape=jax.ShapeDtypeStruct(full.shape, full.dtype),
        grid=(1,),
        in_specs=[pl.BlockSpec((N_META_ROWS, width), lambda i: (0, 0)),
                  pl.BlockSpec(memory_space=pl.ANY)],
        out_specs=pl.BlockSpec((N_META_ROWS, width), lambda i: (ROW_META // N_META_ROWS, 0)),
        input_output_aliases={1: 0},
        compiler_params=_params("arbitrary"),
        name="put_meta_rows",
    )(meta_rows, full)


def _conv_prompt(u, conv_w, conv_b, ln_g, ln_b):
    nt = SEQ // CONV_ROWS
    vec = lambda: pl.BlockSpec((1, CONV_CH), lambda b, t: (0, 0))
    return pl.pallas_call(
        _conv_prompt_kernel,
        out_shape=(jax.ShapeDtypeStruct((N_ROWS, CONV_CH), BF16),
                   jax.ShapeDtypeStruct((N_META_ROWS, CONV_CH), BF16)),
        grid=(BATCH, nt),
        in_specs=[pl.BlockSpec((CONV_ROWS, CONV_CH), lambda b, t: (b * nt + t, 0)),
                  pl.BlockSpec((CONV_HALO, CONV_CH),
                               lambda b, t: (jnp.maximum((b * SEQ + t * CONV_ROWS) // CONV_HALO - 1, 0), 0)),
                  pl.BlockSpec((N_META, CONV_CH), lambda b, t: (ROW_META // N_META + b, 0)),
                  pl.BlockSpec((CONV_WIDTH, CONV_CH), lambda b, t: (0, 0)),
                  vec(), vec(), vec()],
        out_specs=(pl.BlockSpec((CONV_ROWS, CONV_CH), lambda b, t: (b * nt + t, 0)),
                   pl.BlockSpec((N_META, CONV_CH), lambda b, t: (b, 0))),
        scratch_shapes=[pltpu.VMEM((CONV_ROWS + CONV_HALO, CONV_CH), F32)],
        compiler_params=_params("parallel", "arbitrary"),
        name="conv_prompt",
    )(u, u, u, conv_w, conv_b.reshape(1, CONV_CH), ln_g.reshape(1, CONV_CH), ln_b.reshape(1, CONV_CH))


def _conv_sample_kernel(u_ref, st_ref, w_ref, b_ref, g_ref, beta_ref, co_ref, o_ref, ns_ref, full_ref):
    del co_ref
    keep = CONV_WIDTH - 1
    shift0 = CONV_HALO - keep
    ys = []
    for s in range(CONV_SEQS_PER_STEP):
        u = u_ref[s * DEC_SEQ:(s + 1) * DEC_SEQ, :]
        full_ref[0:8, :] = jnp.zeros((8, CONV_CH), F32)
        full_ref[shift0:CONV_HALO, :] = st_ref[s]
        full_ref[CONV_HALO:, :] = u
        parts = [_conv_taps(full_ref[:, l * LANES:(l + 1) * LANES], w_ref, l * LANES, DEC_SEQ)
                 for l in range(CONV_CH // LANES)]
        ys.append(_conv_post(jnp.concatenate(parts, -1), b_ref, g_ref, beta_ref))
        ns_ref[s, 0:keep - DEC_SEQ, :] = st_ref[s, DEC_SEQ:keep, :]
        ns_ref[s, keep - DEC_SEQ:keep, :] = u
    o_ref[...] = jnp.concatenate(ys, 0).astype(BF16)


def _conv_sample(u, conv_full, state, conv_w, conv_b, ln_g, ln_b):
    g = CONV_SEQS_PER_STEP
    rows = g * DEC_SEQ
    r0 = ROW_SAMPLE // rows
    keep = CONV_WIDTH - 1
    vec = lambda: pl.BlockSpec((1, CONV_CH), lambda i: (0, 0))
    return pl.pallas_call(
        _conv_sample_kernel,
        out_shape=(jax.ShapeDtypeStruct((N_ROWS, CONV_CH), BF16),
                   jax.ShapeDtypeStruct((DEC_BATCH, keep, CONV_CH), F32)),
        grid=(DEC_BATCH // g,),
        in_specs=[pl.BlockSpec((rows, CONV_CH), lambda i: (r0 + i, 0)),
                  pl.BlockSpec((g, keep, CONV_CH), lambda i: (i, 0, 0)),
                  pl.BlockSpec((CONV_WIDTH, CONV_CH), lambda i: (0, 0)),
                  vec(), vec(), vec(),
                  pl.BlockSpec(memory_space=pl.ANY)],
        out_specs=(pl.BlockSpec((rows, CONV_CH), lambda i: (r0 + i, 0)),
                   pl.BlockSpec((g, keep, CONV_CH), lambda i: (i, 0, 0))),
        scratch_shapes=[pltpu.VMEM((CONV_HALO + DEC_SEQ, CONV_CH), F32)],
        input_output_aliases={6: 0},
        compiler_params=_params("parallel"),
        name="conv_sample",
    )(u, state, conv_w, conv_b.reshape(1, CONV_CH), ln_g.reshape(1, CONV_CH), ln_b.reshape(1, CONV_CH),
      conv_full)


def _softmax_step(q, kc, vc, m, l, acc, mask=None):
    s = _dot_nt(q, kc) * MLA_SCALE
    if mask is not None:
        s = jnp.where(mask, s, -jnp.inf)
    m_new = jnp.maximum(m, jnp.max(s, -1, keepdims=True))
    alpha = jnp.exp(m - m_new)
    p = jnp.exp(s - m_new)
    l = alpha * l + jnp.sum(p, -1, keepdims=True)
    acc = alpha * acc + _dot(p.astype(BF16), vc)
    return m_new, l, acc


def _causal_mask(n):
    return lax.broadcasted_iota(jnp.int32, (n, n), 0) >= lax.broadcasted_iota(jnp.int32, (n, n), 1)


def _attn_prompt_kernel(q_ref, qm_ref, kn_ref, kp_ref, v_ref, knm_ref, kpm_ref, vm_ref, o_ref, om_ref):
    qi = pl.program_id(2)
    blk = ATTN_BLOCK
    k_meta = jnp.concatenate([knm_ref[...], kpm_ref[...]], -1)
    v_meta = vm_ref[...]

    @pl.when(qi == 0)
    def _():
        init = (jnp.full((N_META, 1), -jnp.inf, F32), jnp.zeros((N_META, 1), F32),
                jnp.zeros((N_META, V_DIM), F32))
        _, l, acc = _softmax_step(qm_ref[...], k_meta, v_meta, *init, mask=_causal_mask(N_META))
        om_ref[...] = (acc / l).astype(BF16)

    q = q_ref[...]
    init = (jnp.full((blk, 1), -jnp.inf, F32), jnp.zeros((blk, 1), F32), jnp.zeros((blk, V_DIM), F32))
    carry = _softmax_step(q, k_meta, v_meta, *init)

    def keys(j):
        r0 = pl.multiple_of(j * blk, blk)
        kc = jnp.concatenate([kn_ref[pl.ds(r0, blk), :], kp_ref[pl.ds(r0, blk), :]], -1)
        return kc, v_ref[pl.ds(r0, blk), :]

    def body(j, c):
        kc, vc = keys(j)
        return _softmax_step(q, kc, vc, *c)

    carry = lax.fori_loop(0, qi, body, carry)
    kc, vc = keys(qi)
    _, l, acc = _softmax_step(q, kc, vc, *carry, mask=_causal_mask(blk))
    o_ref[...] = (acc / l).astype(BF16)


def _attn_prompt(q, kv, kpe):
    nq = SEQ // ATTN_BLOCK
    mrow = ROW_META // N_META
    return pl.pallas_call(
        _attn_prompt_kernel,
        out_shape=(jax.ShapeDtypeStruct((N_ROWS, MLA_HEADS * V_DIM), BF16),
                   jax.ShapeDtypeStruct((N_META_ROWS, MLA_HEADS * V_DIM), BF16)),
        grid=(BATCH, MLA_HEADS, nq),
        in_specs=[pl.BlockSpec((ATTN_BLOCK, HEAD_PAD), lambda b, h, i: (b * nq + i, h)),
                  pl.BlockSpec((N_META, HEAD_PAD), lambda b, h, i: (mrow + b, h)),
                  pl.BlockSpec((SEQ, LANES), lambda b, h, i: (b, h)),
                  pl.BlockSpec((SEQ, LANES), lambda b, h, i: (b, 0)),
                  pl.BlockSpec((SEQ, LANES), lambda b, h, i: (b, MLA_HEADS + h)),
                  pl.BlockSpec((N_META, LANES), lambda b, h, i: (mrow + b, h)),
                  pl.BlockSpec((N_META, LANES), lambda b, h, i: (mrow + b, 0)),
                  pl.BlockSpec((N_META, LANES), lambda b, h, i: (mrow + b, MLA_HEADS + h))],
        out_specs=(pl.BlockSpec((ATTN_BLOCK, V_DIM), lambda b, h, i: (b * nq + i, h)),
                   pl.BlockSpec((N_META, V_DIM), lambda b, h, i: (b, h))),
        compiler_params=_params("parallel", "parallel", "arbitrary"),
        name="attn_prompt",
    )(q, q, kv, kpe, kv, kv, kpe, kv)


def _absorb_q_kernel(q_ref, wuk_ref, qlat_ref, qpe_ref):
    q = q_ref[...]
    qlat = _dot(q[:, :NOPE_DIM], wuk_ref[...])
    qlat_ref[...] = qlat.reshape(DEC_BATCH, DEC_SEQ, KV_RANK)
    qpe_ref[...] = q[:, NOPE_DIM:].astype(F32).reshape(DEC_BATCH, DEC_SEQ, LANES)


def _absorb_q(q_s, wuk_t):
    rows = MLA_HEADS * DEC_SEQ
    return pl.pallas_call(
        _absorb_q_kernel,
        out_shape=(jax.ShapeDtypeStruct((DEC_BATCH, rows, KV_RANK), F32),
                   jax.ShapeDtypeStruct((DEC_BATCH, rows, LANES), F32)),
        grid=(MLA_HEADS,),
        in_specs=[pl.BlockSpec((N_SAMPLE_ROWS, HEAD_PAD), lambda h: (0, h)),
                  pl.BlockSpec((None, NOPE_DIM, KV_RANK), lambda h: (h, 0, 0))],
        out_specs=(pl.BlockSpec((DEC_BATCH, DEC_SEQ, KV_RANK), lambda h: (0, h, 0)),
                   pl.BlockSpec((DEC_BATCH, DEC_SEQ, LANES), lambda h: (0, h, 0))),
        compiler_params=_params("parallel"),
        name="absorb_q",
    )(q_s, wuk_t)


def _decode_kernel(pt_ref, qlat_ref, qpe_ref, *refs):
    g = PAGES_PER_STEP
    c_refs = refs[:g]
    k_refs = refs[g:2 * g]
    cn_ref, kn_ref, o_ref, m_ref, l_ref, acc_ref = refs[2 * g:]
    p_id = pl.program_id(1)
    rows = MLA_HEADS * DEC_SEQ

    @pl.when(p_id == 0)
    def _():
        m_ref[...] = jnp.full((rows, 1), -jnp.inf, F32)
        l_ref[...] = jnp.zeros((rows, 1), F32)
        acc_ref[...] = jnp.zeros((rows, KV_RANK), F32)

    ql = qlat_ref[...].astype(BF16)
    qp = qpe_ref[...][:, :ROPE_DIM].astype(BF16)
    m, l, acc = m_ref[...], l_ref[...], acc_ref[...]
    for g0 in range(0, g, PAGES_PER_GROUP):
        group = range(g0, g0 + PAGES_PER_GROUP)
        cs = [c_refs[i][...].astype(BF16) for i in group]
        s = jnp.concatenate(
            [_dot_nt(ql, c) + _dot(qp, k_refs[i][...].astype(BF16)) for c, i in zip(cs, group)],
            -1) * MLA_SCALE
        m_new = jnp.maximum(m, jnp.max(s, -1, keepdims=True))
        alpha = jnp.exp(m - m_new)
        p = jnp.exp(s - m_new)
        l = alpha * l + jnp.sum(p, -1, keepdims=True)
        pv = _dot(p[:, :PAGE_SIZE].astype(BF16), cs[0])
        for i in range(1, PAGES_PER_GROUP):
            pv = pv + _dot(p[:, i * PAGE_SIZE:(i + 1) * PAGE_SIZE].astype(BF16), cs[i])
        acc = alpha * acc + pv
        m = m_new
    m_ref[...] = m
    l_ref[...] = l
    acc_ref[...] = acc

    @pl.when(p_id == pl.num_programs(1) - 1)
    def _():
        cn = cn_ref[...].astype(BF16)
        kn = kn_ref[...][:, :ROPE_DIM].astype(BF16)
        sn = (_dot_nt(ql, cn) + _dot_nt(qp, kn)) * MLA_SCALE
        tok = lax.broadcasted_iota(jnp.int32, (rows, DEC_SEQ), 0) % DEC_SEQ
        col = lax.broadcasted_iota(jnp.int32, (rows, DEC_SEQ), 1)
        sn = jnp.where(col <= tok, sn, -jnp.inf)
        m_prev = m_ref[...]
        m_fin = jnp.maximum(m_prev, jnp.max(sn, -1, keepdims=True))
        a = jnp.exp(m_prev - m_fin)
        pn = jnp.exp(sn - m_fin)
        l_fin = a * l_ref[...] + jnp.sum(pn, -1, keepdims=True)
        acc = a * acc_ref[...] + _dot(pn.astype(BF16), cn)
        o_ref[...] = acc / l_fin


def _decode_attn(page_table, qlat, qpe, cache_ckv, cache_kpe_t, li, ckv_rows, kpe_rows):
    g = PAGES_PER_STEP
    rows = MLA_HEADS * DEC_SEQ
    srow = ROW_SAMPLE // DEC_SEQ

    def page_spec(shape, j):
        return pl.BlockSpec((None, None) + shape,
                            lambda b, p, pt: (li, pt[b * N_PAGES + p * g + j], 0, 0))

    in_specs = [pl.BlockSpec((None, rows, KV_RANK), lambda b, p, pt: (b, 0, 0)),
                pl.BlockSpec((None, rows, LANES), lambda b, p, pt: (b, 0, 0))]
    in_specs += [page_spec((PAGE_SIZE, KV_RANK), j) for j in range(g)]
    in_specs += [page_spec((ROPE_DIM, PAGE_SIZE), j) for j in range(g)]
    in_specs += [pl.BlockSpec((DEC_SEQ, KV_RANK), lambda b, p, pt: (srow + b, 0)),
                 pl.BlockSpec((DEC_SEQ, LANES), lambda b, p, pt: (srow + b, 0))]
    grid_spec = pltpu.PrefetchScalarGridSpec(
        num_scalar_prefetch=1,
        grid=(DEC_BATCH, N_PAGES // g),
        in_specs=in_specs,
        out_specs=pl.BlockSpec((None, rows, KV_RANK), lambda b, p, pt: (b, 0, 0)),
        scratch_shapes=[pltpu.VMEM((rows, 1), F32), pltpu.VMEM((rows, 1), F32),
                        pltpu.VMEM((rows, KV_RANK), F32)],
    )
    return pl.pallas_call(
        _decode_kernel,
        out_shape=jax.ShapeDtypeStruct((DEC_BATCH, rows, KV_RANK), F32),
        grid_spec=grid_spec,
        compiler_params=_params("parallel", "arbitrary"),
        name="decode_attn",
    )(page_table.reshape(-1), qlat, qpe, *([cache_ckv] * g), *([cache_kpe_t] * g), ckv_rows, kpe_rows)


def _decode_dma_kernel(li, pt_ref, qlat_ref, qpe_ref, ckv_hbm, kpe_hbm, cn_ref, kn_ref, o_ref,
                       cbuf, kbuf, sem, m_ref, l_ref, acc_ref):
    g = PAGES_PER_STEP
    rows = MLA_HEADS * DEC_SEQ
    p_id = pl.program_id(1)
    n_p = pl.num_programs(1)
    step = pl.program_id(0) * n_p + p_id
    n_steps = pl.num_programs(0) * n_p
    slot = lax.rem(step, 2)
    nxt = jnp.where(step + 1 < n_steps, step + 1, 0)

    def copies(st, sl, j):
        page = pt_ref[st * g + j]
        return (pltpu.make_async_copy(ckv_hbm.at[li, page], cbuf.at[sl, j], sem.at[sl, 0]),
                pltpu.make_async_copy(kpe_hbm.at[li, page], kbuf.at[sl, j], sem.at[sl, 1]))

    @pl.when(step == 0)
    def _():
        for j in range(g):
            for cp in copies(0, 0, j):
                cp.start()

    @pl.when(p_id == 0)
    def _():
        m_ref[...] = jnp.full((rows, 1), -jnp.inf, F32)
        l_ref[...] = jnp.zeros((rows, 1), F32)
        acc_ref[...] = jnp.zeros((rows, KV_RANK), F32)

    for j in range(g):
        for cp in copies(step, slot, j):
            cp.wait()

    ql = qlat_ref[...].astype(BF16)
    qp = qpe_ref[...][:, :ROPE_DIM].astype(BF16)
    cs, ss = [], []
    for j in range(g):
        c = cbuf[slot, j].astype(BF16)
        cs.append(c)
        ss.append(_dot_nt(ql, c) + _dot(qp, kbuf[slot, j].astype(BF16)))
        for cp in copies(nxt, 1 - slot, j):
            cp.start()
    s = jnp.concatenate(ss, -1) * MLA_SCALE
    m = m_ref[...]
    m_new = jnp.maximum(m, jnp.max(s, -1, keepdims=True))
    alpha = jnp.exp(m - m_new)
    p = jnp.exp(s - m_new)
    l_ref[...] = alpha * l_ref[...] + jnp.sum(p, -1, keepdims=True)
    pv = _dot(p[:, :PAGE_SIZE].astype(BF16), cs[0])
    for i in range(1, g):
        pv = pv + _dot(p[:, i * PAGE_SIZE:(i + 1) * PAGE_SIZE].astype(BF16), cs[i])
    acc_ref[...] = alpha * acc_ref[...] + pv
    m_ref[...] = m_new

    @pl.when(p_id == n_p - 1)
    def _():
        cn = cn_ref[...].astype(BF16)
        kn = kn_ref[...][:, :ROPE_DIM].astype(BF16)
        sn = (_dot_nt(ql, cn) + _dot_nt(qp, kn)) * MLA_SCALE
        tok = lax.broadcasted_iota(jnp.int32, (rows, DEC_SEQ), 0) % DEC_SEQ
        col = lax.broadcasted_iota(jnp.int32, (rows, DEC_SEQ), 1)
        sn = jnp.where(col <= tok, sn, -jnp.inf)
        m_prev = m_ref[...]
        m_fin = jnp.maximum(m_prev, jnp.max(sn, -1, keepdims=True))
        a = jnp.exp(m_prev - m_fin)
        pn = jnp.exp(sn - m_fin)
        l_fin = a * l_ref[...] + jnp.sum(pn, -1, keepdims=True)
        acc = a * acc_ref[...] + _dot(pn.astype(BF16), cn)
        o_ref[...] = acc / l_fin

    @pl.when(step == n_steps - 1)
    def _():
        for j in range(g):
            for cp in copies(nxt, 1 - slot, j):
                cp.wait()


def _decode_attn_dma(page_table, qlat, qpe, cache_ckv, cache_kpe_t, li, ckv_rows, kpe_rows):
    g = PAGES_PER_STEP
    rows = MLA_HEADS * DEC_SEQ
    srow = ROW_SAMPLE // DEC_SEQ
    grid_spec = pltpu.PrefetchScalarGridSpec(
        num_scalar_prefetch=1,
        grid=(DEC_BATCH, N_PAGES // g),
        in_specs=[pl.BlockSpec((None, rows, KV_RANK), lambda b, p, pt: (b, 0, 0)),
                  pl.BlockSpec((None, rows, LANES), lambda b, p, pt: (b, 0, 0)),
                  pl.BlockSpec(memory_space=pl.ANY),
                  pl.BlockSpec(memory_space=pl.ANY),
                  pl.BlockSpec((DEC_SEQ, KV_RANK), lambda b, p, pt: (srow + b, 0)),
                  pl.BlockSpec((DEC_SEQ, LANES), lambda b, p, pt: (srow + b, 0))],
        out_specs=pl.BlockSpec((None, rows, KV_RANK), lambda b, p, pt: (b, 0, 0)),
        scratch_shapes=[pltpu.VMEM((2, g, PAGE_SIZE, KV_RANK), F32),
                        pltpu.VMEM((2, g, ROPE_DIM, PAGE_SIZE), F32),
                        pltpu.SemaphoreType.DMA((2, 2)),
                        pltpu.VMEM((rows, 1), F32), pltpu.VMEM((rows, 1), F32),
                        pltpu.VMEM((rows, KV_RANK), F32)],
    )
    return pl.pallas_call(
        functools.partial(_decode_dma_kernel, li),
        out_shape=jax.ShapeDtypeStruct((DEC_BATCH, rows, KV_RANK), F32),
        grid_spec=grid_spec,
        compiler_params=_params("arbitrary", "arbitrary"),
        name="decode_attn",
    )(page_table.reshape(-1), qlat, qpe, cache_ckv, cache_kpe_t, ckv_rows, kpe_rows)


def _v_up_kernel(o_ref, w_ref, full_ref, a_ref):
    del full_ref
    rows = V_UP_SEQS_PER_STEP * DEC_SEQ
    outs = []
    for h in range(MLA_HEADS):
        o = o_ref[:, h * DEC_SEQ:(h + 1) * DEC_SEQ, :].reshape(rows, KV_RANK).astype(BF16)
        outs.append(_dot(o, w_ref[h]))
    a_ref[...] = jnp.concatenate(outs, -1).astype(BF16)


def _v_up(o_lat, wuv, attn_full):
    g = V_UP_SEQS_PER_STEP
    rows = g * DEC_SEQ
    r0 = ROW_SAMPLE // rows
    return pl.pallas_call(
        _v_up_kernel,
        out_shape=jax.ShapeDtypeStruct((N_ROWS, MLA_HEADS * V_DIM), BF16),
        grid=(DEC_BATCH // g,),
        in_specs=[pl.BlockSpec((g, MLA_HEADS * DEC_SEQ, KV_RANK), lambda i: (i, 0, 0)),
                  pl.BlockSpec((MLA_HEADS, KV_RANK, V_DIM), lambda i: (0, 0, 0)),
                  pl.BlockSpec(memory_space=pl.ANY)],
        out_specs=pl.BlockSpec((rows, MLA_HEADS * V_DIM), lambda i: (r0 + i, 0)),
        input_output_aliases={2: 0},
        compiler_params=_params("parallel"),
        name="v_up",
    )(o_lat, wuv, attn_full)


def _ret_tables(n, lg):
    i = lax.broadcasted_iota(jnp.int32, (n, 1), 0).astype(F32)
    rel = (lax.broadcasted_iota(jnp.int32, (n, n), 0) - lax.broadcasted_iota(jnp.int32, (n, n), 1)).astype(F32)
    decay = jnp.where(rel >= 0, jnp.exp(lg * jnp.maximum(rel, 0.0)), 0.0)
    q_scale = jnp.exp((i + 1.0) * lg)
    k_scale = jnp.exp((n - 1.0 - i) * lg)
    return decay, q_scale, k_scale


def _ret_chunk(q, k, v, g, s_old, tables, s_decay):
    decay, q_scale, k_scale = tables
    inner = _dot_nt(q, k.astype(BF16)) * decay
    o = _dot(inner.astype(BF16), v) + _dot(q, s_old.astype(BF16)) * q_scale
    s_new = s_decay * s_old + _dot_tn((k * k_scale).astype(BF16), v)
    mu = jnp.mean(o, -1, keepdims=True)
    oc = o - mu
    var = jnp.mean(oc * oc, -1, keepdims=True)
    y = jax.nn.silu(g) * (oc * lax.rsqrt(var + EPS))
    return y, s_new


def _ret_prompt_kernel(c_ref, q_ref, k_ref, v_ref, g_ref, qm_ref, km_ref, vm_ref, gm_ref,
                       y_ref, ym_ref, so_ref, s_ref):
    h = pl.program_id(1)
    lg = c_ref[h, 0]
    ym, s0 = _ret_chunk(qm_ref[...], km_ref[...], vm_ref[...], gm_ref[...],
                        jnp.zeros((RET_DK, RET_DV), F32), _ret_tables(N_META, lg), c_ref[h, 2])
    ym_ref[...] = ym.astype(BF16)
    s_ref[...] = s0
    tables = _ret_tables(RET_CHUNK, lg)
    s_decay = c_ref[h, 1]

    def body(c, carry):
        r0 = pl.multiple_of(c * RET_CHUNK, RET_CHUNK)
        rows = pl.ds(r0, RET_CHUNK)
        y, s_new = _ret_chunk(q_ref[rows, :], k_ref[rows, :], v_ref[rows, :], g_ref[rows, :],
                              s_ref[...], tables, s_decay)
        y_ref[rows, :] = y.astype(BF16)
        s_ref[...] = s_new
        return carry

    lax.fori_loop(0, SEQ // RET_CHUNK, body, 0, unroll=2)
    so_ref[...] = s_ref[...]


def _ret_prompt(consts, q, k, v, g):
    mrow = ROW_META // N_META
    seq = lambda w: pl.BlockSpec((SEQ, w), lambda b, h: (b, h))
    meta = lambda w: pl.BlockSpec((N_META, w), lambda b, h: (mrow + b, h))
    return pl.pallas_call(
        _ret_prompt_kernel,
        out_shape=(jax.ShapeDtypeStruct((N_ROWS, RET_HV), BF16),
                   jax.ShapeDtypeStruct((N_META_ROWS, RET_HV), BF16),
                   jax.ShapeDtypeStruct((BATCH, RET_HEADS, RET_DK, RET_DV), F32)),
        grid=(BATCH, RET_HEADS),
        in_specs=[pl.BlockSpec(memory_space=pltpu.SMEM),
                  seq(RET_DK), seq(RET_DK), seq(RET_DV), seq(RET_DV),
                  meta(RET_DK), meta(RET_DK), meta(RET_DV), meta(RET_DV)],
        out_specs=(seq(RET_DV), pl.BlockSpec((N_META, RET_DV), lambda b, h: (b, h)),
                   pl.BlockSpec((None, None, RET_DK, RET_DV), lambda b, h: (b, h, 0, 0))),
        scratch_shapes=[pltpu.VMEM((RET_DK, RET_DV), F32)],
        compiler_params=_params("parallel", "parallel"),
        name="ret_prompt",
    )(consts, q, k, v, g, q, k, v, g)


def _ret_sample_kernel(c_ref, q_ref, k_ref, v_ref, g_ref, s_ref, *rest):
    y_ref, so_ref = rest[-2:]
    h = pl.program_id(1)
    tables = _ret_tables(DEC_SEQ, c_ref[h, 0])
    s_decay = c_ref[h, 3]
    q_all = q_ref[...].astype(F32)
    k_all = k_ref[...]
    v_all = v_ref[...].astype(F32)
    g_all = g_ref[...]
    ys = []
    for s in range(RET_SEQS_PER_STEP):
        rows = slice(s * DEC_SEQ, (s + 1) * DEC_SEQ)
        y, s_new = _ret_chunk(q_all[rows].astype(BF16), k_all[rows], v_all[rows].astype(BF16), g_all[rows],
                              s_ref[s], tables, s_decay)
        ys.append(y)
        so_ref[s] = s_new
    y_ref[...] = jnp.concatenate(ys, 0).astype(BF16)


def _ret_sample(consts, q, k, v, g, y_full, state_all, o, new_states):
    gs = RET_SEQS_PER_STEP
    rows = gs * DEC_SEQ
    r0 = ROW_SAMPLE // rows
    tok = lambda w: pl.BlockSpec((rows, w), lambda i, h: (r0 + i, h))
    st = pl.BlockSpec((None, gs, None, RET_DK, RET_DV), lambda i, h: (o, i, h, 0, 0))
    in_specs = [pl.BlockSpec(memory_space=pltpu.SMEM), tok(RET_DK), tok(RET_DK), tok(RET_DV), tok(RET_DV), st,
                pl.BlockSpec(memory_space=pl.ANY)]
    args = [consts, q, k, v, g, state_all, y_full]
    aliases = {6: 0}
    if new_states is not None:
        in_specs.append(pl.BlockSpec(memory_space=pl.ANY))
        aliases[len(args)] = 1
        args.append(new_states)
    return pl.pallas_call(
        _ret_sample_kernel,
        out_shape=(jax.ShapeDtypeStruct((N_ROWS, RET_HV), BF16),
                   jax.ShapeDtypeStruct(state_all.shape, F32)),
        grid=(DEC_BATCH // gs, RET_HEADS),
        in_specs=in_specs,
        out_specs=(tok(RET_DV), st),
        input_output_aliases=aliases,
        compiler_params=_params("parallel", "parallel"),
        name="ret_sample",
    )(*args)


def _rotate_half_cols(w):
    half = w.shape[-1] // 2
    return jnp.concatenate([-w[..., half:], w[..., :half]], -1)


def _pad_cols(w, n):
    return jnp.pad(w, [(0, 0)] * (w.ndim - 1) + [(0, n - w.shape[-1])])


def _even_weights(w_in, w_uq, w_kv_up):
    conv_cols = 2 * CONV_CH
    w_glu = w_in[:, :conv_cols].astype(BF16)
    w_kpe = w_in[:, conv_cols + Q_RANK + KV_RANK:]
    w_lat = jnp.concatenate([w_in[:, conv_cols:conv_cols + Q_RANK + KV_RANK],
                             _pad_cols(w_kpe, LANES), _pad_cols(_rotate_half_cols(w_kpe), LANES)], -1).astype(BF16)
    w_q = _pad_cols(w_uq, HEAD_PAD).reshape(Q_RANK, MLA_HEADS * HEAD_PAD).astype(BF16)
    w_qr = _pad_cols(_rotate_half_cols(w_uq[..., NOPE_DIM:]), LANES).reshape(Q_RANK, MLA_HEADS * LANES).astype(BF16)
    w_uk = w_kv_up[..., :NOPE_DIM]
    w_uv = w_kv_up[..., NOPE_DIM:]
    w_kv = jnp.concatenate([w_uk.reshape(KV_RANK, -1), w_uv.reshape(KV_RANK, -1)], -1).astype(BF16)
    w_uk_t = jnp.transpose(w_uk, (1, 2, 0)).astype(BF16)
    w_uv_h = jnp.transpose(w_uv, (1, 0, 2)).astype(BF16)
    return w_glu, w_lat, w_q, w_qr, w_kv, w_uk_t, w_uv_h


def _rope_tables(pos, d):
    inv = ROPE_BASE ** (-jnp.arange(d // 2, dtype=F32) * 2.0 / d)
    ang = pos.astype(F32)[:, None] * inv
    return jnp.cos(ang), jnp.sin(ang)


def _ret_consts():
    lg = jnp.log1p(-(2.0 ** (-5.0 - jnp.arange(RET_HEADS, dtype=F32))))
    return jnp.stack([lg, jnp.exp(RET_CHUNK * lg), jnp.exp(N_META * lg), jnp.exp(DEC_SEQ * lg)], -1)


def kernel(x_prompt, x_sample, cache_ckv, cache_kpe, state_conv, state_ret, page_table, meta_tokens,
           norm_mix, norm_ffn, norm_final, w_in_even, conv_w, conv_b, conv_ln_g, conv_ln_b,
           mla_q_norm, mla_w_uq, mla_kv_norm, mla_w_kv_up, w_out_even, w_in_odd, w_out_odd,
           ffn_w_gate, ffn_w_up, ffn_w_down):
    h = jnp.concatenate([x_prompt.reshape(N_PROMPT_ROWS, D_MODEL),
                         jnp.tile(meta_tokens.astype(x_prompt.dtype), (BATCH, 1)),
                         x_sample.reshape(N_SAMPLE_ROWS, D_MODEL)], 0)
    pos = jnp.concatenate([jnp.tile(N_META + jnp.arange(SEQ), BATCH),
                           jnp.tile(jnp.arange(N_META), BATCH),
                           jnp.tile(PAST_LEN + jnp.arange(DEC_SEQ), DEC_BATCH)])
    cos_m, sin_m = _rope_tables(pos, ROPE_DIM)
    cos_mla = _pad_cols(jnp.concatenate([cos_m, cos_m], -1), LANES)
    sin_mla = _pad_cols(jnp.concatenate([sin_m, sin_m], -1), LANES)
    cos_ret, sin_ret = _rope_tables(pos, RET_DK)
    ret_consts = _ret_consts()

    cache_kpe_t = jnp.swapaxes(cache_kpe, 2, 3)
    w_out_even_b = w_out_even.astype(BF16)
    w_out_odd_b = w_out_odd.astype(BF16)
    ffn_w_down_b = ffn_w_down.astype(BF16)

    p_ckv, p_kpe, s_ckv, s_kpe, p_conv, s_conv, p_ret = [], [], [], [], [], [], []
    s_ret = None
    xn = _rmsnorm_rows(h, norm_mix[0])
    for layer in range(DEPTH):
        if layer % 2 == 0:
            e = layer // 2
            w_glu, w_lat, w_q, w_qr, w_kv, w_uk_t, w_uv_h = _even_weights(w_in_even[e], mla_w_uq[e], mla_w_kv_up[e])
            u = _glu_proj(xn, w_glu)
            q, kv, ckv, kpe, kpe_b = _latent_proj(xn, w_lat, mla_q_norm[e], mla_kv_norm[e],
                                                  w_q, w_qr, w_kv, cos_mla, sin_mla)
            conv_out, co_m = _conv_prompt(u, conv_w[e], conv_b[e], conv_ln_g[e], conv_ln_b[e])
            conv_out = _put_meta_rows(conv_out, co_m)
            conv_out, new_state = _conv_sample(u, conv_out, state_conv[e],
                                               conv_w[e], conv_b[e], conv_ln_g[e], conv_ln_b[e])
            attn, at_m = _attn_prompt(q, kv, kpe_b)
            attn = _put_meta_rows(attn, at_m)
            qlat, qpe = _absorb_q(q[ROW_SAMPLE:], w_uk_t)
            o_lat = _decode_attn_dma(page_table, qlat, qpe, cache_ckv, cache_kpe_t, e, ckv, kpe)
            attn = _v_up(o_lat, w_uv_h, attn)
            h, xn = _out_proj([conv_out, attn], w_out_even_b, e, h, norm_ffn[layer], 1, BF16)
            u_p = u[:N_PROMPT_ROWS].reshape(BATCH, SEQ, CONV_CH)
            p_conv.append(u_p[:, SEQ - (CONV_WIDTH - 1):])
            s_conv.append(new_state)
            ckv_p = ckv[:N_PROMPT_ROWS].reshape(BATCH, SEQ, KV_RANK)
            ckv_m = ckv[ROW_META:ROW_SAMPLE].reshape(BATCH, N_META, KV_RANK)
            p_ckv.append(jnp.concatenate([ckv_m, ckv_p], 1))
            kpe_p = kpe[:N_PROMPT_ROWS, :ROPE_DIM].reshape(BATCH, SEQ, ROPE_DIM)
            kpe_m = kpe[ROW_META:ROW_SAMPLE, :ROPE_DIM].reshape(BATCH, N_META, ROPE_DIM)
            p_kpe.append(jnp.concatenate([kpe_m, kpe_p], 1))
            s_ckv.append(ckv[ROW_SAMPLE:].reshape(DEC_BATCH, DEC_SEQ, KV_RANK))
            s_kpe.append(kpe[ROW_SAMPLE:, :ROPE_DIM].reshape(DEC_BATCH, DEC_SEQ, ROPE_DIM))
        else:
            o = layer // 2
            q, k = _ret_qk_proj(xn, w_in_odd, o, cos_ret, sin_ret)
            v, g = _ret_vg_proj(xn, w_in_odd, o)
            y, y_m, st_p = _ret_prompt(ret_consts, q, k, v, g)
            y = _put_meta_rows(y, y_m)
            y, s_ret = _ret_sample(ret_consts, q, k, v, g, y, state_ret, o, s_ret)
            h, xn = _out_proj([y], w_out_odd_b, o, h, norm_ffn[layer], 2, BF16)
            p_ret.append(st_p)
        act = _ffn_up(xn, ffn_w_gate, ffn_w_up, layer)
        last = layer == DEPTH - 1
        gain = norm_final if last else norm_mix[layer + 1]
        h, xn = _out_proj([act], ffn_w_down_b, layer, h, gain, 4, F32 if last else BF16)
    y_prompt = xn[:N_PROMPT_ROWS].reshape(BATCH, SEQ, D_MODEL)
    y_sample = xn[ROW_SAMPLE:].reshape(DEC_BATCH, DEC_SEQ, D_MODEL)
    return (y_prompt, y_sample, jnp.stack(p_ckv), jnp.stack(p_kpe), jnp.stack(s_ckv), jnp.stack(s_kpe),
            jnp.stack(p_conv), jnp.stack(s_conv), jnp.stack(p_ret), s_ret)
```
